```python
import jax, jax.numpy as jnp
from jax import lax
import numpy as np

D_MODEL = 1024
BATCH = 16
SEQ = 2048
DEPTH = 2

EPS = 1e-6
D_FF = 2816
CONV_WIDTH = 4
D_RG = D_MODEL // 2
RG_BLOCKS = 8
RG_C = 8.0
ML_HEADS = 4
D_ML = D_MODEL // 2
ML_HEAD_DIM = D_ML // ML_HEADS
ML_CHUNK = 128
D_AB_IN = 2 * D_RG + 4 * D_ML + 2 * ML_HEADS
NSA_HEADS = 16
NSA_GROUPS = 4
NSA_HEAD_DIM = 64
NSA_KV = NSA_GROUPS * NSA_HEAD_DIM
NSA_WIDTH = NSA_HEADS * NSA_HEAD_DIM
CMP_BLOCK = 32
CMP_STRIDE = 16
CMP_HIDDEN = 256
SEL_BLOCK = 64
SEL_TOPN = 8
WINDOW = 256
NSA_QBLOCK = 64
NSA_BRANCHES = 3
D_NSA_IN = NSA_WIDTH + 6 * NSA_KV + NSA_BRANCHES * NSA_HEADS
FORCED_SCORE = 1e6
N_EVEN = (DEPTH + 1) // 2
N_ODD = DEPTH // 2

kernel_name = "hybrid_rglru_mlstm_nsa_macaron"


def rmsnorm(x, g):
    xf = x.astype(jnp.float32)
    y = xf * lax.rsqrt(jnp.mean(xf * xf, axis=-1, keepdims=True) + EPS)
    return (y * g.astype(jnp.float32)).astype(x.dtype)


def swiglu_ffn(x, w_gu, w_down):
    gate, up = jnp.split(x @ w_gu, 2, axis=-1)
    return (jax.nn.silu(gate) * up) @ w_down


def causal_dwconv(x, w, b):
    k, c = w.shape
    y = lax.conv_general_dilated(x, w[:, None, :], window_strides=(1,), padding=[(k - 1, 0)],
                                 dimension_numbers=('NWC', 'WIO', 'NWC'), feature_group_count=c)
    return y + b


def masked_softmax(s, mask):
    s = jnp.where(mask, s.astype(jnp.float32), -1e30)
    return jax.nn.softmax(s, axis=-1) * mask


def rglru(x, w_r, b_r, w_i, b_i, lam):
    bsz, s, c = x.shape
    xb = x.reshape(bsz, s, RG_BLOCKS, c // RG_BLOCKS)
    r = jax.nn.sigmoid(jnp.einsum('bsnc,ncd->bsnd', xb, w_r).reshape(bsz, s, c) + b_r)
    ig = jax.nn.sigmoid(jnp.einsum('bsnc,ncd->bsnd', xb, w_i).reshape(bsz, s, c) + b_i)
    log_a = -RG_C * r.astype(jnp.float32) * jax.nn.softplus(-lam.astype(jnp.float32))
    a = jnp.exp(log_a)
    u = jnp.sqrt(-jnp.expm1(2.0 * log_a)) * (ig * x).astype(jnp.float32)

    def combine(left, right):
        a1, b1 = left
        a2, b2 = right
        return a1 * a2, a2 * b1 + b2

    _, h = lax.associative_scan(combine, (a, u), axis=1)
    return h.astype(x.dtype)


def mlstm_chunkwise(q, k, v, log_i, log_f):
    bsz, nh, s, dh = q.shape
    nc = s // ML_CHUNK
    q = q.reshape(bsz, nh, nc, ML_CHUNK, dh) * (dh ** -0.5)
    k = k.reshape(bsz, nh, nc, ML_CHUNK, dh)
    v = v.reshape(bsz, nh, nc, ML_CHUNK, dh)
    log_i = log_i.reshape(bsz, nh, nc, ML_CHUNK)
    b = jnp.cumsum(log_f.reshape(bsz, nh, nc, ML_CHUNK), axis=-1)
    g = b[..., -1]
    w = g[..., None] - b + log_i
    m_loc = jnp.max(w, axis=-1)
    wk = jnp.exp(w - m_loc[..., None])
    c_loc = jnp.einsum('bhcl,bhclv,bhclk->bhcvk', wk, v, k)
    n_loc = jnp.einsum('bhcl,bhclk->bhck', wk, k)

    def step(carry, xs):
        c_st, n_st, m_st = carry
        c_l, n_l, m_l, g_c = xs
        m_new = jnp.maximum(g_c + m_st, m_l)
        sa = jnp.exp(g_c + m_st - m_new)
        sb = jnp.exp(m_l - m_new)
        new = (sa[..., None, None] * c_st + sb[..., None, None] * c_l,
               sa[..., None] * n_st + sb[..., None] * n_l, m_new)
        return new, carry

    init = (jnp.zeros((bsz, nh, dh, dh), jnp.float32), jnp.zeros((bsz, nh, dh), jnp.float32),
            jnp.full((bsz, nh), -1e30, jnp.float32))
    xs = (jnp.moveaxis(c_loc, 2, 0), jnp.moveaxis(n_loc, 2, 0),
          jnp.moveaxis(m_loc, 2, 0), jnp.moveaxis(g, 2, 0))
    _, (c0, n0, m0) = lax.scan(step, init, xs)
    c0 = jnp.moveaxis(c0, 0, 2)
    n0 = jnp.moveaxis(n0, 0, 2)
    m0 = jnp.moveaxis(m0, 0, 2)
    causal = jnp.tril(jnp.ones((ML_CHUNK, ML_CHUNK), bool))
    d = jnp.where(causal, b[..., :, None] - b[..., None, :] + log_i[..., None, :], -jnp.inf)
    m_inter = b + m0[..., None]
    m = jnp.maximum(m_inter, jnp.max(d, axis=-1))
    p = jnp.exp(d - m[..., None]) * jnp.einsum('bhcjd,bhckd->bhcjk', q, k)
    sc = jnp.exp(m_inter - m)
    num = sc[..., None] * jnp.einsum('bhcvk,bhcjk->bhcjv', c0, q) + jnp.einsum('bhcjk,bhckv->bhcjv', p, v)
    den = sc * jnp.einsum('bhck,bhcjk->bhcj', n0, q) + jnp.sum(p, axis=-1)
    h = num / jnp.maximum(jnp.abs(den), jnp.exp(-m))[..., None]
    return h.reshape(bsz, nh, s, dh)


def mlstm_block(qk, v, o, i_pre, f_pre, conv_w, conv_b, b_i, b_f, norm_g):
    bsz, s, _ = v.shape
    qk = jax.nn.silu(causal_dwconv(qk, conv_w, conv_b))
    q, k = jnp.split(qk, 2, axis=-1)
    heads = lambda t: t.reshape(bsz, s, ML_HEADS, ML_HEAD_DIM).transpose(0, 2, 1, 3).astype(jnp.float32)
    log_i = (i_pre + b_i).astype(jnp.float32).transpose(0, 2, 1)
    log_f = jax.nn.log_sigmoid((f_pre + b_f).astype(jnp.float32)).transpose(0, 2, 1)
    h = mlstm_chunkwise(heads(q), heads(k), heads(v), log_i, log_f)
    h = h * lax.rsqrt(jnp.mean(h * h, axis=-1, keepdims=True) + EPS)
    h = h.transpose(0, 2, 1, 3).reshape(bsz, s, D_ML) * norm_g.astype(jnp.float32)
    return (jax.nn.sigmoid(o.astype(jnp.float32)) * h).astype(v.dtype)


def ab_mixer(xn, w_in, rg_conv_w, rg_conv_b, rg_w_r, rg_b_r, rg_w_i, rg_b_i, rg_lambda,
             ml_conv_w, ml_conv_b, ml_b_i, ml_b_f, ml_norm, w_out):
    proj = xn @ w_in
    cuts = [D_RG, 2 * D_RG, 2 * D_RG + 2 * D_ML, 2 * D_RG + 3 * D_ML, 2 * D_RG + 4 * D_ML,
            2 * D_RG + 4 * D_ML + ML_HEADS]
    xa, ga, qk, v, o, i_pre, f_pre = jnp.split(proj, cuts, axis=-1)
    ya = jax.nn.gelu(ga) * rglru(causal_dwconv(xa, rg_conv_w, rg_conv_b),
                                 rg_w_r, rg_b_r, rg_w_i, rg_b_i, rg_lambda)
    yb = mlstm_block(qk, v, o, i_pre, f_pre, ml_conv_w, ml_conv_b, ml_b_i, ml_b_f, ml_norm)
    return jnp.concatenate([ya, yb], axis=-1) @ w_out


def compress_blocks(kv, pe, w1, b1, w2):
    bsz, s, g, dk = kv.shape
    nb = (s - CMP_BLOCK) // CMP_STRIDE + 1
    idx = np.arange(nb)[:, None] * CMP_STRIDE + np.arange(CMP_BLOCK)[None, :]
    blk = kv[:, idx] + pe[:, None, :]
    blk = blk.transpose(0, 1, 3, 2, 4).reshape(bsz, nb, g, CMP_BLOCK * dk)
    return jax.nn.gelu(blk @ w1 + b1) @ w2


def cmp_to_sel_matrix(n_cmp, n_sel):
    cs = np.arange(n_cmp)[:, None] * CMP_STRIDE
    ss = np.arange(n_sel)[None, :] * SEL_BLOCK
    ov = np.clip(np.minimum(cs + CMP_BLOCK, ss + SEL_BLOCK) - np.maximum(cs, ss), 0, None)
    return jnp.asarray(ov / CMP_BLOCK, dtype=jnp.float32)


def nsa_mixer(xn, w_in, pe_k, k_w1, k_b1, k_w2, pe_v, v_w1, v_b1, v_w2, b_gate, w_out):
    bsz, s, _ = xn.shape
    g, r, dk = NSA_GROUPS, NSA_HEADS // NSA_GROUPS, NSA_HEAD_DIM
    proj = xn @ w_in
    cuts = list(np.cumsum([NSA_WIDTH] + [NSA_KV] * 6))
    q, kc, vc, ks, vs, kw, vw, gates = jnp.split(proj, cuts, axis=-1)
    q = q.reshape(bsz, s, g, r, dk) * (dk ** -0.5)
    kvh = lambda t: t.reshape(bsz, s, g, dk)
    kc = compress_blocks(kvh(kc), pe_k, k_w1, k_b1, k_w2)
    vc = compress_blocks(kvh(vc), pe_v, v_w1, v_b1, v_w2)
    n_cmp = kc.shape[1]
    n_sel = s // SEL_BLOCK
    top_n = min(SEL_TOPN, n_sel)
    m_sel = cmp_to_sel_matrix(n_cmp, n_sel)
    cmp_end = jnp.arange(n_cmp) * CMP_STRIDE + CMP_BLOCK - 1
    sel_blk = jnp.arange(n_sel)
    ks = kvh(ks).transpose(0, 2, 1, 3)
    vs = kvh(vs).transpose(0, 2, 1, 3)
    pad = jnp.zeros((bsz, WINDOW, g, dk), xn.dtype)
    kw = jnp.concatenate([pad, kvh(kw)], axis=1)
    vw = jnp.concatenate([pad, kvh(vw)], axis=1)
    gates = jax.nn.sigmoid(gates.reshape(bsz, s, g, r, NSA_BRANCHES) + b_gate.reshape(g, r, NSA_BRANCHES))

    def query_block(s0):
        t = s0 + jnp.arange(NSA_QBLOCK)
        qb = lax.dynamic_slice_in_dim(q, s0, NSA_QBLOCK, axis=1)
        gb = lax.dynamic_slice_in_dim(gates, s0, NSA_QBLOCK, axis=1)
        sc = jnp.einsum('bqgrd,bngd->bgrqn', qb, kc)
        p_cmp = masked_softmax(sc, cmp_end[None, :] <= t[:, None])
        o_cmp = jnp.einsum('bgrqn,bngd->bqgrd', p_cmp.astype(vc.dtype), vc)
        score = jnp.einsum('bgqn,nj->bgqj', jnp.sum(p_cmp, axis=2), m_sel)
        cur = (t // SEL_BLOCK)[:, None]
        valid = sel_blk[None, :] * SEL_BLOCK <= t[:, None]
        forced = (sel_blk[None, :] == 0) | (sel_blk[None, :] == cur) | (sel_blk[None, :] == cur - 1)
        score = jnp.where(forced & valid, FORCED_SCORE, jnp.where(valid, score, -1.0))
        _, idx = lax.top_k(score, top_n)
        key_idx = (idx[..., None] * SEL_BLOCK + jnp.arange(SEL_BLOCK)).reshape(bsz, g, -1)
        n_keys = top_n * SEL_BLOCK
        k_sel = jnp.take_along_axis(ks, key_idx[..., None], axis=2).reshape(bsz, g, NSA_QBLOCK, n_keys, dk)
        v_sel = jnp.take_along_axis(vs, key_idx[..., None], axis=2).reshape(bsz, g, NSA_QBLOCK, n_keys, dk)
        ss = jnp.einsum('bqgrd,bgqkd->bgrqk', qb, k_sel)
        sel_mask = key_idx.reshape(bsz, g, 1, NSA_QBLOCK, n_keys) <= t[:, None]
        p_slc = masked_softmax(ss, sel_mask)
        o_slc = jnp.einsum('bgrqk,bgqkd->bqgrd', p_slc.astype(v_sel.dtype), v_sel)
        k_win = lax.dynamic_slice_in_dim(kw, s0, WINDOW + NSA_QBLOCK, axis=1)
        v_win = lax.dynamic_slice_in_dim(vw, s0, WINDOW + NSA_QBLOCK, axis=1)
        pos = s0 - WINDOW + jnp.arange(WINDOW + NSA_QBLOCK)
        win_mask = (pos[None, :] <= t[:, None]) & (pos[None, :] > t[:, None] - WINDOW) & (pos[None, :] >= 0)
        sw = jnp.einsum('bqgrd,bkgd->bgrqk', qb, k_win)
        p_win = masked_softmax(sw, win_mask)
        o_win = jnp.einsum('bgrqk,bkgd->bqgrd', p_win.astype(v_win.dtype), v_win)
        out = gb[..., 0:1] * o_cmp + gb[..., 1:2] * o_slc + gb[..., 2:3] * o_win
        return out.reshape(bsz, NSA_QBLOCK, NSA_WIDTH)

    starts = jnp.arange(s // NSA_QBLOCK) * NSA_QBLOCK
    out = lax.map(query_block, starts)
    out = out.transpose(1, 0, 2, 3).reshape(bsz, s, NSA_WIDTH)
    return out @ w_out


def setup_inputs(seed: int = 0) -> dict:
    key = jax.random.key(seed)
    keys = iter(jax.random.split(key, 40))
    nrm = lambda shape, scale: scale * jax.random.normal(next(keys), shape, jnp.float32)
    gain = lambda shape: 1.0 + 0.05 * jax.random.normal(next(keys), shape, jnp.float32)
    a8 = jax.random.uniform(next(keys), (N_EVEN, D_RG), jnp.float32, 0.9, 0.999)
    a0 = a8 ** (1.0 / RG_C)
    rg_lambda = jnp.log(a0) - jnp.log1p(-a0)
    ml_b_f = jnp.linspace(3.0, 6.0, ML_HEADS, dtype=jnp.float32)[None, :] + nrm((N_EVEN, ML_HEADS), 0.1)
    flat = CMP_BLOCK * NSA_HEAD_DIM
    return {
        "x": nrm((BATCH, SEQ, D_MODEL), 1.0),
        "ffn1_norm": gain((DEPTH, D_MODEL)),
        "ffn1_w_gu": nrm((DEPTH, D_MODEL, 2 * D_FF), D_MODEL ** -0.5),
        "ffn1_w_down": nrm((DEPTH, D_FF, D_MODEL), D_FF ** -0.5),
        "mix_norm": gain((DEPTH, D_MODEL)),
        "ffn2_norm": gain((DEPTH, D_MODEL)),
        "ffn2_w_gu": nrm((DEPTH, D_MODEL, 2 * D_FF), D_MODEL ** -0.5),
        "ffn2_w_down": nrm((DEPTH, D_FF, D_MODEL), D_FF ** -0.5),
        "ab_w_in": nrm((N_EVEN, D_MODEL, D_AB_IN), D_MODEL ** -0.5),
        "rg_conv_w": nrm((N_EVEN, CONV_WIDTH, D_RG), CONV_WIDTH ** -0.5),
        "rg_conv_b": nrm((N_EVEN, D_RG), 0.02),
        "rg_w_r": nrm((N_EVEN, RG_BLOCKS, D_RG // RG_BLOCKS, D_RG // RG_BLOCKS), (D_RG // RG_BLOCKS) ** -0.5),
        "rg_b_r": nrm((N_EVEN, D_RG), 0.02),
        "rg_w_i": nrm((N_EVEN, RG_BLOCKS, D_RG // RG_BLOCKS, D_RG // RG_BLOCKS), (D_RG // RG_BLOCKS) ** -0.5),
        "rg_b_i": nrm((N_EVEN, D_RG), 0.02),
        "rg_lambda": rg_lambda,
        "ml_conv_w": nrm((N_EVEN, CONV_WIDTH, 2 * D_ML), CONV_WIDTH ** -0.5),
        "ml_conv_b": nrm((N_EVEN, 2 * D_ML), 0.02),
        "ml_b_i": nrm((N_EVEN, ML_HEADS), 0.1),
        "ml_b_f": ml_b_f,
        "ml_norm": gain((N_EVEN, D_ML)),
        "ab_w_out": nrm((N_EVEN, D_RG + D_ML, D_MODEL), (D_RG + D_ML) ** -0.5),
        "nsa_w_in": nrm((N_ODD, D_MODEL, D_NSA_IN), D_MODEL ** -0.5),
        "nsa_pe_k": nrm((N_ODD, CMP_BLOCK, NSA_HEAD_DIM), 0.1),
        "nsa_k_w1": nrm((N_ODD, flat, CMP_HIDDEN), flat ** -0.5),
        "nsa_k_b1": nrm((N_ODD, CMP_HIDDEN), 0.02),
        "nsa_k_w2": nrm((N_ODD, CMP_HIDDEN, NSA_HEAD_DIM), CMP_HIDDEN ** -0.5),
        "nsa_pe_v": nrm((N_ODD, CMP_BLOCK, NSA_HEAD_DIM), 0.1),
        "nsa_v_w1": nrm((N_ODD, flat, CMP_HIDDEN), flat ** -0.5),
        "nsa_v_b1": nrm((N_ODD, CMP_HIDDEN), 0.02),
        "nsa_v_w2": nrm((N_ODD, CMP_HIDDEN, NSA_HEAD_DIM), CMP_HIDDEN ** -0.5),
        "nsa_b_gate": nrm((N_ODD, NSA_BRANCHES * NSA_HEADS), 0.1),
        "nsa_w_out": nrm((N_ODD, NSA_WIDTH, D_MODEL), NSA_WIDTH ** -0.5),
        "final_norm": gain((D_MODEL,)),
    }


def reference(x, ffn1_norm, ffn1_w_gu, ffn1_w_down, mix_norm, ffn2_norm, ffn2_w_gu, ffn2_w_down,
              ab_w_in, rg_conv_w, rg_conv_b, rg_w_r, rg_b_r, rg_w_i, rg_b_i, rg_lambda,
              ml_conv_w, ml_conv_b, ml_b_i, ml_b_f, ml_norm, ab_w_out,
              nsa_w_in, nsa_pe_k, nsa_k_w1, nsa_k_b1, nsa_k_w2, nsa_pe_v, nsa_v_w1, nsa_v_b1, nsa_v_w2,
              nsa_b_gate, nsa_w_out, final_norm):
    for i in range(DEPTH):
        j = i // 2
        x = x + 0.5 * swiglu_ffn(rmsnorm(x, ffn1_norm[i]), ffn1_w_gu[i], ffn1_w_down[i])
        xn = rmsnorm(x, mix_norm[i])
        if i % 2 == 0:
            x = x + ab_mixer(xn, ab_w_in[j], rg_conv_w[j], rg_conv_b[j], rg_w_r[j], rg_b_r[j],
                             rg_w_i[j], rg_b_i[j], rg_lambda[j], ml_conv_w[j], ml_conv_b[j],
                             ml_b_i[j], ml_b_f[j], ml_norm[j], ab_w_out[j])
        else:
            x = x + nsa_mixer(xn, nsa_w_in[j], nsa_pe_k[j], nsa_k_w1[j], nsa_k_b1[j], nsa_k_w2[j],
                              nsa_pe_v[j], nsa_v_w1[j], nsa_v_b1[j], nsa_v_w2[j], nsa_b_gate[j], nsa_w_out[j])
        x = x + 0.5 * swiglu_ffn(rmsnorm(x, ffn2_norm[i]), ffn2_w_gu[i], ffn2_w_down[i])
    return rmsnorm(x, final_norm)
```

```python
import functools

import jax
import jax.numpy as jnp
import numpy as np
from jax import lax
from jax.experimental import pallas as pl
from jax.experimental.pallas import tpu as pltpu

F32 = jnp.float32
BF16 = jnp.bfloat16
HIGHEST = lax.Precision.HIGHEST

EPS = 1e-6
RG_C = 8.0
RG_BLOCKS = 8
CONV_WIDTH = 4
ML_HEADS = 4
ML_CHUNK = 128
NSA_HEADS = 16
NSA_GROUPS = 4
NSA_HEAD_DIM = 64
NSA_BRANCHES = 3
CMP_BLOCK = 32
CMP_STRIDE = 16
SEL_BLOCK = 64
SEL_TOPN = 8
WINDOW = 256
FORCED_SCORE = 1e6

LANES = 128
SUBLANES = 8
NEG = -1e30
VMEM_LIMIT = 48 * 1024 * 1024

TOKEN_TILE = 512
NSA_QTILE = 256


def _params(n_axes):
    return pltpu.CompilerParams(dimension_semantics=("arbitrary",) * n_axes,
                                vmem_limit_bytes=VMEM_LIMIT)


def _resident(shape, index_map):
    return pl.BlockSpec(shape, index_map, pipeline_mode=pl.Buffered(1))


def _rms(x, g):
    return x * lax.rsqrt(jnp.mean(x * x, axis=-1, keepdims=True) + EPS) * g


def _gelu_tanh(x):
    return 0.5 * x * (1.0 + jnp.tanh(0.7978845608028654 * (x + 0.044715 * (x * x * x))))


def _softplus(z):
    return jnp.maximum(z, 0.0) + jnp.log1p(jnp.exp(-jnp.abs(z)))


def _ffn_kernel(x_ref, g_ref, wg_ref, wu_ref, wd_ref, gf_ref, o_ref, *, n_chunks, final):
    x = x_ref[...]
    xn = _rms(x, g_ref[...]).astype(BF16)
    fc = wg_ref.shape[1] // n_chunks
    acc = None
    for c in range(n_chunks):
        gate = jnp.dot(xn, wg_ref[:, c * fc:(c + 1) * fc], preferred_element_type=F32)
        up = jnp.dot(xn, wu_ref[:, c * fc:(c + 1) * fc], preferred_element_type=F32)
        h = (gate * jax.nn.sigmoid(gate) * up).astype(BF16)
        part = jnp.dot(h, wd_ref[c * fc:(c + 1) * fc, :], preferred_element_type=F32)
        acc = part if acc is None else acc + part
    y = x + 0.5 * acc
    if final:
        y = _rms(y, gf_ref[...])
    o_ref[...] = y


def _ffn(x, g, w_gu, w_down, gf, *, final):
    n, d = x.shape
    f = w_down.shape[0]
    tm = min(TOKEN_TILE, n)
    const = lambda i: (0, 0)
    return pl.pallas_call(
        functools.partial(_ffn_kernel, n_chunks=2, final=final),
        grid=(n // tm,),
        in_specs=[pl.BlockSpec((tm, d), lambda i: (i, 0)),
                  pl.BlockSpec((1, d), const),
                  _resident((d, f), const),
                  _resident((d, f), lambda i: (0, 1)),
                  _resident((f, d), const),
                  pl.BlockSpec((1, d), const)],
        out_specs=pl.BlockSpec((tm, d), lambda i: (i, 0)),
        out_shape=jax.ShapeDtypeStruct((n, d), F32),
        compiler_params=_params(1),
        name="ffn_final" if final else "ffn",
    )(x, g.reshape(1, d), w_gu, w_gu, w_down, gf.reshape(1, d))


def _norm_matmul_kernel(x_ref, g_ref, w_ref, o_ref):
    xn = _rms(x_ref[...], g_ref[...]).astype(BF16)
    o_ref[...] = jnp.dot(xn, w_ref[...], preferred_element_type=F32)


def _norm_matmul(x, g, w):
    n, d = x.shape
    dout = w.shape[1]
    tm = min(TOKEN_TILE, n)
    const = lambda i: (0, 0)
    return pl.pallas_call(
        _norm_matmul_kernel,
        grid=(n // tm,),
        in_specs=[pl.BlockSpec((tm, d), lambda i: (i, 0)),
                  pl.BlockSpec((1, d), const),
                  _resident((d, dout), const)],
        out_specs=pl.BlockSpec((tm, dout), lambda i: (i, 0)),
        out_shape=jax.ShapeDtypeStruct((n, dout), F32),
        compiler_params=_params(1),
        name="norm_matmul",
    )(x, g.reshape(1, d), w)


def _resid_matmul_kernel(*refs, n_in):
    x_ref = refs[0]
    a_refs = refs[1:1 + n_in]
    w_refs = refs[1 + n_in:1 + 2 * n_in]
    o_ref = refs[1 + 2 * n_in]
    y = x_ref[...]
    for a_ref, w_ref in zip(a_refs, w_refs):
        y = y + jnp.dot(a_ref[...].astype(BF16), w_ref[...], preferred_element_type=F32)
    o_ref[...] = y


def _resid_matmul(x, acts, ws):
    n, d = x.shape
    tm = min(TOKEN_TILE, n)
    const = lambda i: (0, 0)
    row = lambda i: (i, 0)
    return pl.pallas_call(
        functools.partial(_resid_matmul_kernel, n_in=len(acts)),
        grid=(n // tm,),
        in_specs=[pl.BlockSpec((tm, d), row)]
        + [pl.BlockSpec((tm, a.shape[1]), row) for a in acts]
        + [_resident(w.shape, const) for w in ws],
        out_specs=pl.BlockSpec((tm, d), row),
        out_shape=jax.ShapeDtypeStruct((n, d), F32),
        compiler_params=_params(1),
        name="resid_matmul",
    )(x, *acts, *ws)


def _causal_conv(x, cw, cb, pad_ref):
    s = x.shape[0]
    pad_ref[0:SUBLANES, :] = jnp.zeros((SUBLANES, x.shape[1]), F32)
    pad_ref[SUBLANES:SUBLANES + s, :] = x
    y = cb + cw[CONV_WIDTH - 1:CONV_WIDTH, :] * x
    for j in range(CONV_WIDTH - 1):
        off = SUBLANES - (CONV_WIDTH - 1 - j)
        y = y + cw[j:j + 1, :] * pad_ref[off:off + s, :]
    return y


def _rglru_kernel(xa_ref, ga_ref, cw_ref, cb_ref, wr_ref, br_ref, wi_ref, bi_ref, lam_ref,
                  o_ref, pad_ref, a_ref, u_ref):
    s = xa_ref.shape[1]
    ng = s // SUBLANES
    xc = _causal_conv(xa_ref[0], cw_ref[...], cb_ref[...], pad_ref)
    xb = xc.astype(BF16)
    r = jax.nn.sigmoid(jnp.dot(xb, wr_ref[...], preferred_element_type=F32) + br_ref[...])
    ig = jax.nn.sigmoid(jnp.dot(xb, wi_ref[...], preferred_element_type=F32) + bi_ref[...])
    log_a = -RG_C * r * _softplus(-lam_ref[...])
    a = jnp.exp(log_a)
    th = jnp.tanh(log_a)
    u = jnp.sqrt(-2.0 * th / (1.0 - th)) * (ig * xc)
    a3 = a.reshape(ng, SUBLANES, LANES)
    u3 = u.reshape(ng, SUBLANES, LANES)
    row = lax.broadcasted_iota(jnp.int32, (ng, SUBLANES, LANES), 1)
    sh = 1
    while sh < SUBLANES:
        a_s = pltpu.roll(a3, sh, 1)
        u_s = pltpu.roll(u3, sh, 1)
        m = row >= sh
        u3 = jnp.where(m, a3 * u_s + u3, u3)
        a3 = jnp.where(m, a3 * a_s, a3)
        sh *= 2
    a_ref[...] = a3
    u_ref[...] = u3

    def body(i, h):
        hh = a_ref[i] * h + u_ref[i]
        u_ref[i] = hh
        return hh[SUBLANES - 1:SUBLANES, :]

    lax.fori_loop(0, ng, body, jnp.zeros((1, LANES), F32), unroll=8)
    o_ref[0] = _gelu_tanh(ga_ref[0]) * u_ref[...].reshape(s, LANES)


def _rglru(proj3, cw, cb, wr_bd, br, wi_bd, bi, lam):
    bsz, s, _ = proj3.shape
    c = cw.shape[1]
    nb = c // LANES
    vec = lambda b, j: (0, j)
    return pl.pallas_call(
        _rglru_kernel,
        grid=(bsz, nb),
        in_specs=[pl.BlockSpec((1, s, LANES), lambda b, j: (b, 0, j)),
                  pl.BlockSpec((1, s, LANES), lambda b, j: (b, 0, nb + j)),
                  pl.BlockSpec((CONV_WIDTH, LANES), vec),
                  pl.BlockSpec((1, LANES), vec),
                  pl.BlockSpec((LANES, LANES), lambda b, j: (j, j)),
                  pl.BlockSpec((1, LANES), vec),
                  pl.BlockSpec((LANES, LANES), lambda b, j: (j, j)),
                  pl.BlockSpec((1, LANES), vec),
                  pl.BlockSpec((1, LANES), vec)],
        out_specs=pl.BlockSpec((1, s, LANES), lambda b, j: (b, 0, j)),
        out_shape=jax.ShapeDtypeStruct((bsz, s, c), F32),
        scratch_shapes=[pltpu.VMEM((s + SUBLANES, LANES), F32),
                        pltpu.VMEM((s // SUBLANES, SUBLANES, LANES), F32),
                        pltpu.VMEM((s // SUBLANES, SUBLANES, LANES), F32)],
        compiler_params=_params(2),
        name="rglru",
    )(proj3, proj3, cw, cb.reshape(1, c), wr_bd, br.reshape(1, c), wi_bd, bi.reshape(1, c),
      lam.reshape(1, c))


def _mlstm_kernel(bias_ref, q_ref, k_ref, v_ref, og_ref, gi_ref, gf_ref, cwq_ref, cbq_ref,
                  cwk_ref, cbk_ref, ng_ref, o_ref, pad_ref, q_sc, k_sc, li_sc, lf_sc, bc_sc):
    hd = pl.program_id(1)
    s = q_ref.shape[1]
    dh = q_ref.shape[2]
    nc = s // ML_CHUNK
    q_sc[...] = jax.nn.silu(_causal_conv(q_ref[0], cwq_ref[...], cbq_ref[...], pad_ref)) * (dh ** -0.5)
    k_sc[...] = jax.nn.silu(_causal_conv(k_ref[0], cwk_ref[...], cbk_ref[...], pad_ref))
    li_sc[...] = gi_ref[0, 0] + bias_ref[hd]
    lf = -_softplus(-(gf_ref[0, 0] + bias_ref[ML_HEADS + hd]))
    lf_sc[...] = lf
    jj = lax.broadcasted_iota(jnp.int32, (ML_CHUNK, ML_CHUNK), 0)
    kk = lax.broadcasted_iota(jnp.int32, (ML_CHUNK, ML_CHUNK), 1)
    tri = kk <= jj
    eye = kk == jj
    upper = jnp.where(jj <= kk, 1.0, 0.0).astype(F32)
    bc_sc[...] = jnp.dot(lf, upper, precision=HIGHEST, preferred_element_type=F32)
    norm_g = ng_ref[...]

    def chunk(c, carry):
        c_st, n_st, m_st = carry
        r0 = pl.multiple_of(c * ML_CHUNK, ML_CHUNK)
        qc = q_sc[pl.ds(r0, ML_CHUNK), :]
        kc = k_sc[pl.ds(r0, ML_CHUNK), :]
        vc = v_ref[0, pl.ds(r0, ML_CHUNK), :]
        li_row = li_sc[pl.ds(c, 1), :]
        lf_row = lf_sc[pl.ds(c, 1), :]
        b_row = bc_sc[pl.ds(c, 1), :]
        b_col = jnp.sum(jnp.where(tri, lf_row, 0.0), axis=1, keepdims=True)
        li_col = jnp.sum(jnp.where(eye, li_row, 0.0), axis=1, keepdims=True)
        g_tot = jnp.sum(lf_row, axis=1, keepdims=True)
        w_col = g_tot - b_col + li_col
        m_loc = jnp.max(w_col, axis=0, keepdims=True)
        kw = kc * jnp.exp(w_col - m_loc)
        c_loc = jnp.dot(vc.T.astype(BF16), kw.astype(BF16), preferred_element_type=F32)
        n_loc = jnp.sum(kw, axis=0, keepdims=True)
        d = jnp.where(tri, b_col - b_row + li_row, NEG)
        m_inter = b_col + m_st
        m = jnp.maximum(m_inter, jnp.max(d, axis=1, keepdims=True))
        qb = qc.astype(BF16)
        qk = lax.dot_general(qb, kc.astype(BF16), (((1,), (1,)), ((), ())), preferred_element_type=F32)
        p = jnp.exp(d - m) * qk
        sc = jnp.exp(m_inter - m)
        inter = lax.dot_general(qb, c_st.astype(BF16), (((1,), (1,)), ((), ())), preferred_element_type=F32)
        num = sc * inter + jnp.dot(p.astype(BF16), vc.astype(BF16), preferred_element_type=F32)
        den = sc * jnp.sum(qc * n_st, axis=1, keepdims=True) + jnp.sum(p, axis=1, keepdims=True)
        h = num / jnp.maximum(jnp.abs(den), jnp.exp(-m))
        h = h * lax.rsqrt(jnp.mean(h * h, axis=1, keepdims=True) + EPS) * norm_g
        o_ref[0, pl.ds(r0, ML_CHUNK), :] = jax.nn.sigmoid(og_ref[0, pl.ds(r0, ML_CHUNK), :]) * h
        m_new = jnp.maximum(g_tot + m_st, m_loc)
        sa = jnp.exp(g_tot + m_st - m_new)
        sb = jnp.exp(m_loc - m_new)
        return sa * c_st + sb * c_loc, sa * n_st + sb * n_loc, m_new

    init = (jnp.zeros((dh, dh), F32), jnp.zeros((1, dh), F32), jnp.full((1, 1), NEG, F32))
    lax.fori_loop(0, nc, chunk, init)


def _mlstm(proj3, gates_t, bias, cw, cb, norm_g, *, col0):
    bsz, s, _ = proj3.shape
    nh = ML_HEADS
    dh = LANES
    nc = s // ML_CHUNK
    base = col0 // dh
    blk = lambda off: pl.BlockSpec((1, s, dh), lambda b, h: (b, 0, base + off + h))
    vec = lambda off: (lambda b, h: (0, off + h))
    return pl.pallas_call(
        _mlstm_kernel,
        grid=(bsz, nh),
        in_specs=[pl.BlockSpec(memory_space=pltpu.SMEM),
                  blk(0), blk(nh), blk(2 * nh), blk(3 * nh),
                  pl.BlockSpec((1, 1, nc, ML_CHUNK), lambda b, h: (b, h, 0, 0)),
                  pl.BlockSpec((1, 1, nc, ML_CHUNK), lambda b, h: (b, nh + h, 0, 0)),
                  pl.BlockSpec((CONV_WIDTH, dh), vec(0)), pl.BlockSpec((1, dh), vec(0)),
                  pl.BlockSpec((CONV_WIDTH, dh), vec(nh)), pl.BlockSpec((1, dh), vec(nh)),
                  pl.BlockSpec((1, dh), vec(0))],
        out_specs=pl.BlockSpec((1, s, dh), lambda b, h: (b, 0, h)),
        out_shape=jax.ShapeDtypeStruct((bsz, s, nh * dh), F32),
        scratch_shapes=[pltpu.VMEM((s + SUBLANES, dh), F32),
                        pltpu.VMEM((s, dh), F32), pltpu.VMEM((s, dh), F32),
                        pltpu.VMEM((nc, ML_CHUNK), F32), pltpu.VMEM((nc, ML_CHUNK), F32),
                        pltpu.VMEM((nc, ML_CHUNK), F32)],
        compiler_params=_params(2),
        name="mlstm",
    )(bias, proj3, proj3, proj3, proj3, gates_t, gates_t, cw, cb.reshape(1, -1), cw,
      cb.reshape(1, -1), norm_g.reshape(1, -1))


def _ab_mixer(x, g, w_in, rg_conv_w, rg_conv_b, rg_w_r, rg_b_r, rg_w_i, rg_b_i, rg_lambda,
              ml_conv_w, ml_conv_b, ml_b_i, ml_b_f, ml_norm, w_out, bsz, s):
    d_rg = rg_conv_w.shape[1]
    d_ml = ml_norm.shape[0]
    d_main = 2 * d_rg + 4 * d_ml
    n_gate = 2 * ML_HEADS
    w_in_p = jnp.pad(w_in, ((0, 0), (0, LANES - n_gate))).astype(BF16)
    proj = _norm_matmul(x, g, w_in_p)
    proj3 = proj.reshape(bsz, s, proj.shape[1])
    bd = lambda w: jax.scipy.linalg.block_diag(*[w[i] for i in range(RG_BLOCKS)]).astype(BF16)
    ya = _rglru(proj3, rg_conv_w, rg_conv_b, bd(rg_w_r), rg_b_r, bd(rg_w_i), rg_b_i, rg_lambda)
    gates_t = proj3[:, :, d_main:d_main + n_gate].transpose(0, 2, 1).reshape(
        bsz, n_gate, s // ML_CHUNK, ML_CHUNK)
    yb = _mlstm(proj3, gates_t, jnp.concatenate([ml_b_i, ml_b_f]), ml_conv_w, ml_conv_b, ml_norm,
                col0=2 * d_rg)
    w_out_b = w_out.astype(BF16)
    return _resid_matmul(x, [ya.reshape(bsz * s, d_rg), yb.reshape(bsz * s, d_ml)],
                         [w_out_b[:d_rg], w_out_b[d_rg:]])


def _compress_kernel(r_ref, pet_ref, peb_ref, w1t_ref, w1b_ref, b1_ref, w2_ref, o_ref):
    r = r_ref[0, 0]
    nr = r.shape[0]
    top = jnp.dot((r + pet_ref[...]).astype(BF16), w1t_ref[...], preferred_element_type=F32)
    bot = jnp.dot((r + peb_ref[...]).astype(BF16), w1b_ref[...], preferred_element_type=F32)
    pre = top + pltpu.roll(bot, nr - 1, 0) + b1_ref[...]
    out = jnp.dot(_gelu_tanh(pre).astype(BF16), w2_ref[...], preferred_element_type=F32)
    rows = lax.broadcasted_iota(jnp.int32, out.shape, 0)
    o_ref[0, 0] = jnp.where(rows < nr - 1, out, 0.0).astype(o_ref.dtype)


def _compress(kv_rows, pe, w1, b1, w2_wide):
    bsz, ng, nr, width = kv_rows.shape
    hid = w1.shape[1]
    half = CMP_STRIDE * NSA_HEAD_DIM
    const = lambda b, g: (0, 0)
    w1b = w1.astype(BF16)
    return pl.pallas_call(
        _compress_kernel,
        grid=(bsz, ng),
        in_specs=[pl.BlockSpec((1, 1, nr, width), lambda b, g: (b, g, 0, 0)),
                  pl.BlockSpec((1, half), const), pl.BlockSpec((1, half), const),
                  _resident((half, hid), const), _resident((half, hid), const),
                  pl.BlockSpec((1, hid), const), _resident((hid, LANES), const)],
        out_specs=pl.BlockSpec((1, 1, nr, LANES), lambda b, g: (b, g, 0, 0)),
        out_shape=jax.ShapeDtypeStruct((bsz, ng, nr, LANES), BF16),
        compiler_params=_params(2),
        name="nsa_compress",
    )(kv_rows, pe[:CMP_STRIDE].reshape(1, half), pe[CMP_STRIDE:].reshape(1, half),
      w1b[:half], w1b[half:], b1.reshape(1, hid), w2_wide.astype(BF16))


def _nsa_kernel(q_ref, gt_ref, bg_ref, kc_ref, vc_ref, ks_ref, vs_ref, kw_ref, vw_ref, msel_ref,
                o_ref, m_sc, l_sc, acc_sc):
    grp = pl.program_id(1)
    qi = pl.program_id(2)
    tq = q_ref.shape[0]
    rep = NSA_HEADS // NSA_GROUPS
    rows = rep * tq
    half = LANES // 2
    nt = (((1,), (1,)), ((), ()))
    s0 = qi * tq

    lane_q = lax.broadcasted_iota(jnp.int32, (tq, LANES), 1)
    lo_half = lane_q < half
    qs = q_ref[...] * (NSA_HEAD_DIM ** -0.5)
    parts = []
    for r in range(rep):
        piece = qs[:, (r // 2) * LANES:(r // 2 + 1) * LANES]
        if r % 2:
            piece = pltpu.roll(piece, half, 1)
        parts.append(jnp.where(lo_half, piece, 0.0))
    qp = jnp.concatenate(parts, axis=0)
    qpb = qp.astype(BF16)
    lane = lax.broadcasted_iota(jnp.int32, (rows, LANES), 1)
    t_row = s0 + lax.rem(lax.broadcasted_iota(jnp.int32, (rows, LANES), 0), tq)

    n_cmp = kc_ref.shape[2] - 1
    sc = lax.dot_general(qpb, kc_ref[0, 0], nt, preferred_element_type=F32)
    cmask = (lane * CMP_STRIDE + (CMP_BLOCK - 1) <= t_row) & (lane < n_cmp)
    scm = jnp.where(cmask, sc, NEG)
    e = jnp.where(cmask, jnp.exp(scm - jnp.max(scm, axis=1, keepdims=True)), 0.0)
    den = jnp.sum(e, axis=1, keepdims=True)
    p_cmp = e / jnp.where(den > 0.0, den, 1.0)
    o_cmp = jnp.dot(p_cmp.astype(BF16), vc_ref[0, 0], preferred_element_type=F32)

    psum = p_cmp[0:tq]
    for r in range(1, rep):
        psum = psum + p_cmp[r * tq:(r + 1) * tq]
    score_t = lax.dot_general(msel_ref[...], psum, nt, precision=HIGHEST, preferred_element_type=F32)
    n_sel = 32
    st = score_t[half:half + n_sel, :]
    jb = lax.broadcasted_iota(jnp.int32, (n_sel, tq), 0)
    cur = lax.shift_right_logical(s0 + lax.broadcasted_iota(jnp.int32, (n_sel, tq), 1), 6)
    valid = jb <= cur
    forced = (jb == 0) | (jb == cur) | (jb == cur - 1)
    s2 = jnp.where(forced & valid, FORCED_SCORE, jnp.where(valid, st, -1.0))
    rank = jnp.zeros((n_sel, tq), jnp.int32)
    for i in range(n_sel):
        si = s2[i:i + 1, :]
        beats = (si > s2) | ((si == s2) & (jb > i))
        rank = rank + jnp.where(beats, 1, 0)
    ind = jnp.where((rank < SEL_TOPN) & valid, 1.0, 0.0)
    ind_full = jnp.concatenate([jnp.zeros((half, tq), F32), ind, jnp.zeros((LANES - half - n_sel, tq), F32)],
                               axis=0)
    ind_t = ind_full.T
    bias = jnp.where((lane_q >= half) & (lane_q < half + n_sel), (ind_t - 1.0) * (-NEG), 0.0)
    q_aug = (qp + jnp.concatenate([bias] * rep, axis=0)).astype(BF16)

    m_sc[...] = jnp.full(m_sc.shape, NEG, F32)
    l_sc[...] = jnp.zeros(l_sc.shape, F32)
    acc_sc[...] = jnp.zeros(acc_sc.shape, F32)

    def sel_step(kt, diagonal):
        k0 = pl.multiple_of(kt * tq, tq)
        s = lax.dot_general(q_aug, ks_ref[0, 0, pl.ds(k0, tq), :], nt, preferred_element_type=F32)
        if diagonal:
            pos = k0 + lax.broadcasted_iota(jnp.int32, (rows, tq), 1)
            s = jnp.where(pos <= t_row[:, 0:1], s, NEG)
        m_old = m_sc[...]
        m_new = jnp.maximum(m_old, jnp.max(s, axis=1, keepdims=True))
        alpha = jnp.exp(m_old - m_new)
        p = jnp.exp(s - m_new)
        l_sc[...] = alpha * l_sc[...] + jnp.sum(p, axis=1, keepdims=True)
        acc_sc[...] = alpha * acc_sc[...] + jnp.dot(p.astype(BF16), vs_ref[0, 0, pl.ds(k0, tq), :],
                                                    preferred_element_type=F32)
        m_sc[...] = m_new

    def sel_body(kt, carry):
        sel_step(kt, False)
        return carry

    lax.fori_loop(0, qi, sel_body, 0)
    sel_step(qi, True)
    o_slc = acc_sc[...] / l_sc[...]

    t_col = t_row[:, 0:1]
    kp = jnp.maximum(qi - 1, 0)
    p0 = pl.multiple_of(kp * tq, tq)
    d0 = pl.multiple_of(qi * tq, tq)
    col = lax.broadcasted_iota(jnp.int32, (rows, tq), 1)
    s_prev = lax.dot_general(qpb, kw_ref[0, 0, pl.ds(p0, tq), :], nt, preferred_element_type=F32)
    s_prev = jnp.where((p0 + col > t_col - WINDOW) & (qi >= 1), s_prev, NEG)
    s_diag = lax.dot_general(qpb, kw_ref[0, 0, pl.ds(d0, tq), :], nt, preferred_element_type=F32)
    s_diag = jnp.where(d0 + col <= t_col, s_diag, NEG)
    mw = jnp.maximum(jnp.max(s_prev, axis=1, keepdims=True), jnp.max(s_diag, axis=1, keepdims=True))
    p_prev = jnp.exp(s_prev - mw)
    p_diag = jnp.exp(s_diag - mw)
    lw = jnp.sum(p_prev, axis=1, keepdims=True) + jnp.sum(p_diag, axis=1, keepdims=True)
    o_win = (jnp.dot(p_prev.astype(BF16), vw_ref[0, 0, pl.ds(p0, tq), :], preferred_element_type=F32)
             + jnp.dot(p_diag.astype(BF16), vw_ref[0, 0, pl.ds(d0, tq), :], preferred_element_type=F32)) / lw

    gate = jax.nn.sigmoid(gt_ref[...] + bg_ref[...])
    merged = []
    for r in range(rep):
        tot = None
        for br, o_br in enumerate((o_cmp, o_slc, o_win)):
            idx = grp * (rep * NSA_BRANCHES) + r * NSA_BRANCHES + br
            gcol = jnp.sum(jnp.where(lane_q == idx, gate, 0.0), axis=1, keepdims=True)
            term = gcol * o_br[r * tq:(r + 1) * tq]
            tot = term if tot is None else tot + term
        merged.append(tot)
    for pair in range(rep // 2):
        o_ref[:, pair * LANES:(pair + 1) * LANES] = jnp.where(lo_half, merged[2 * pair], merged[2 * pair + 1])


def _nsa_attention(proj, b_gate, kc, vc, ks_aug, vs2, kw_p, vw2, msel_t, bsz, s):
    n, dproj = proj.shape
    tq = NSA_QTILE
    assert tq == WINDOW and s % tq == 0 and s // SEL_BLOCK == 32
    rep = NSA_HEADS // NSA_GROUPS
    qw = rep * NSA_HEAD_DIM
    width = NSA_HEADS * NSA_HEAD_DIM
    nq = s // tq
    gate_blk = (dproj - LANES) // LANES
    per_bg = lambda shape: pl.BlockSpec((1, 1) + shape, lambda b, g, i: (b, g, 0, 0))
    const = lambda b, g, i: (0, 0)
    rows = rep * tq
    return pl.pallas_call(
        _nsa_kernel,
        grid=(bsz, NSA_GROUPS, nq),
        in_specs=[pl.BlockSpec((tq, qw), lambda b, g, i: (b * nq + i, g)),
                  pl.BlockSpec((tq, LANES), lambda b, g, i: (b * nq + i, gate_blk)),
                  pl.BlockSpec((1, LANES), const),
                  per_bg(kc.shape[2:]), per_bg(vc.shape[2:]),
                  per_bg(ks_aug.shape[2:]), per_bg(vs2.shape[2:]),
                  per_bg(kw_p.shape[2:]), per_bg(vw2.shape[2:]),
                  pl.BlockSpec(msel_t.shape, const)],
        out_specs=pl.BlockSpec((tq, qw), lambda b, g, i: (b * nq + i, g)),
        out_shape=jax.ShapeDtypeStruct((n, width), F32),
        scratch_shapes=[pltpu.VMEM((rows, 1), F32), pltpu.VMEM((rows, 1), F32),
                        pltpu.VMEM((rows, LANES), F32)],
        compiler_params=_params(3),
        name="nsa_attention",
    )(proj, proj, b_gate, kc, vc, ks_aug, vs2, kw_p, vw2, msel_t)


def _sel_matrix_t(n_rows, n_sel):
    n_cmp = np.arange(n_rows)[None, :] * CMP_STRIDE
    sel = np.arange(n_sel)[:, None] * SEL_BLOCK
    ov = np.clip(np.minimum(n_cmp + CMP_BLOCK, sel + SEL_BLOCK) - np.maximum(n_cmp, sel), 0, None)
    out = np.zeros((LANES, n_rows), np.float32)
    out[LANES // 2:LANES // 2 + n_sel] = ov / CMP_BLOCK
    return jnp.asarray(out)


def _nsa_mixer(x, g, w_in, pe_k, k_w1, k_b1, k_w2, pe_v, v_w1, v_b1, v_w2, b_gate, w_out, bsz, s):
    dk = NSA_HEAD_DIM
    ng = NSA_GROUPS
    width = NSA_HEADS * dk
    kv = ng * dk
    n_gate = NSA_BRANCHES * NSA_HEADS
    w_in_p = jnp.pad(w_in, ((0, 0), (0, LANES - n_gate))).astype(BF16)
    proj = _norm_matmul(x, g, w_in_p)
    proj3 = proj.reshape(bsz, s, proj.shape[1])

    def group_major(i):
        t = proj3[:, :, width + i * kv:width + (i + 1) * kv]
        return t.reshape(bsz, s, ng, dk).transpose(0, 2, 1, 3)

    kc_in, vc_in, ks, vs, kw, vw = [group_major(i) for i in range(6)]
    nr = s // CMP_STRIDE
    kc = _compress(kc_in.reshape(bsz, ng, nr, CMP_STRIDE * dk), pe_k, k_w1, k_b1,
                   jnp.pad(k_w2, ((0, 0), (0, LANES - dk))))
    vc = _compress(vc_in.reshape(bsz, ng, nr, CMP_STRIDE * dk), pe_v, v_w1, v_b1,
                   jnp.concatenate([v_w2, v_w2], axis=1))
    n_sel = s // SEL_BLOCK
    onehot = jax.nn.one_hot(jnp.arange(s) // SEL_BLOCK, n_sel, dtype=F32)
    onehot = jnp.broadcast_to(onehot, (bsz, ng, s, n_sel))
    zeros = lambda w: jnp.zeros((bsz, ng, s, w), F32)
    ks_aug = jnp.concatenate([ks, onehot, zeros(LANES - dk - n_sel)], axis=-1).astype(BF16)
    vs2 = jnp.concatenate([vs, vs], axis=-1).astype(BF16)
    kw_p = jnp.concatenate([kw, zeros(LANES - dk)], axis=-1).astype(BF16)
    vw2 = jnp.concatenate([vw, vw], axis=-1).astype(BF16)
    bg = jnp.pad(b_gate, (0, LANES - n_gate)).reshape(1, LANES)
    attn = _nsa_attention(proj, bg, kc, vc, ks_aug, vs2, kw_p, vw2, _sel_matrix_t(nr, n_sel), bsz, s)
    return _resid_matmul(x, [attn], [w_out.astype(BF16)])


def kernel(x, ffn1_norm, ffn1_w_gu, ffn1_w_down, mix_norm, ffn2_norm, ffn2_w_gu, ffn2_w_down,
           ab_w_in, rg_conv_w, rg_conv_b, rg_w_r, rg_b_r, rg_w_i, rg_b_i, rg_lambda,
           ml_conv_w, ml_conv_b, ml_b_i, ml_b_f, ml_norm, ab_w_out,
           nsa_w_in, nsa_pe_k, nsa_k_w1, nsa_k_b1, nsa_k_w2, nsa_pe_v, nsa_v_w1, nsa_v_b1, nsa_v_w2,
           nsa_b_gate, nsa_w_out, final_norm):
    bsz, s, d = x.shape
    depth = ffn1_norm.shape[0]
    h = x.reshape(bsz * s, d)
    for i in range(depth):
        j = i // 2
        h = _ffn(h, ffn1_norm[i], ffn1_w_gu[i].astype(BF16), ffn1_w_down[i].astype(BF16),
                 final_norm, final=False)
        if i % 2 == 0:
            h = _ab_mixer(h, mix_norm[i], ab_w_in[j], rg_conv_w[j], rg_conv_b[j], rg_w_r[j], rg_b_r[j],
                          rg_w_i[j], rg_b_i[j], rg_lambda[j], ml_conv_w[j], ml_conv_b[j], ml_b_i[j],
                          ml_b_f[j], ml_norm[j], ab_w_out[j], bsz, s)
        else:
            h = _nsa_mixer(h, mix_norm[i], nsa_w_in[j], nsa_pe_k[j], nsa_k_w1[j], nsa_k_b1[j],
                           nsa_k_w2[j], nsa_pe_v[j], nsa_v_w1[j], nsa_v_b1[j], nsa_v_w2[j],
                           nsa_b_gate[j], nsa_w_out[j], bsz, s)
        h = _ffn(h, ffn2_norm[i], ffn2_w_gu[i].astype(BF16), ffn2_w_down[i].astype(BF16),
                 final_norm, final=(i == depth - 1))
    return h.reshape(bsz, s, d)
```

```python
import functools

import jax
import jax.numpy as jnp
import numpy as np
from jax import lax
from jax.experimental import pallas as pl
from jax.experimental.pallas import tpu as pltpu

F32 = jnp.float32
BF16 = jnp.bfloat16
HIGHEST = lax.Precision.HIGHEST

EPS = 1e-6
RG_C = 8.0
RG_BLOCKS = 8
CONV_WIDTH = 4
ML_HEADS = 4
ML_CHUNK = 128
NSA_HEADS = 16
NSA_GROUPS = 4
NSA_HEAD_DIM = 64
NSA_BRANCHES = 3
CMP_BLOCK = 32
CMP_STRIDE = 16
SEL_BLOCK = 64
SEL_TOPN = 8
WINDOW = 256
FORCED_SCORE = 1e6

LANES = 128
SUBLANES = 8
NEG = -1e30
VMEM_LIMIT = 48 * 1024 * 1024

TOKEN_TILE = 512
NSA_QTILE = 256
NSA_VROWS = 80
LOG2E = 1.4426950408889634
NT_DIMS = (((1,), (1,)), ((), ()))


def _params(n_axes):
    return pltpu.CompilerParams(dimension_semantics=("arbitrary",) * n_axes,
                                vmem_limit_bytes=VMEM_LIMIT)


def _resident(shape, index_map):
    return pl.BlockSpec(shape, index_map, pipeline_mode=pl.Buffered(1))


def _rms(x, g):
    return x * lax.rsqrt(jnp.mean(x * x, axis=-1, keepdims=True) + EPS) * g


def _gelu_tanh(x):
    return 0.5 * x * (1.0 + jnp.tanh(0.7978845608028654 * (x + 0.044715 * (x * x * x))))


def _softplus(z):
    return jnp.maximum(z, 0.0) + jnp.log1p(jnp.exp(-jnp.abs(z)))


def _ffn_kernel(*refs, n_mix, n_chunks, final):
    x_ref = refs[0]
    a_refs = refs[1:1 + n_mix]
    wo_refs = refs[1 + n_mix:1 + 2 * n_mix]
    g_ref, wg_ref, wu_ref, wd_ref, gf_ref, o_ref = refs[1 + 2 * n_mix:]
    x = x_ref[...]
    for a_ref, wo_ref in zip(a_refs, wo_refs):
        x = x + jnp.dot(a_ref[...], wo_ref[...], preferred_element_type=F32)
    xn = _rms(x, g_ref[...]).astype(BF16)
    fc = wg_ref.shape[1] // n_chunks
    acc = None
    for c in range(n_chunks):
        gate = jnp.dot(xn, wg_ref[:, c * fc:(c + 1) * fc], preferred_element_type=F32)
        up = jnp.dot(xn, wu_ref[:, c * fc:(c + 1) * fc], preferred_element_type=F32)
        h = (gate * jax.nn.sigmoid(gate) * up).astype(BF16)
        part = jnp.dot(h, wd_ref[c * fc:(c + 1) * fc, :], preferred_element_type=F32)
        acc = part if acc is None else acc + part
    y = x + 0.5 * acc
    if final:
        y = _rms(y, gf_ref[...])
    o_ref[...] = y


def _ffn(x, mix, g, w_gu, w_down, gf, *, final):
    n, d = x.shape
    f = w_down.shape[0]
    tm = min(TOKEN_TILE, n)
    const = lambda i: (0, 0)
    row = lambda i: (i, 0)
    acts = [a for a, _ in mix]
    wos = [w for _, w in mix]
    return pl.pallas_call(
        functools.partial(_ffn_kernel, n_mix=len(mix), n_chunks=2, final=final),
        grid=(n // tm,),
        in_specs=[pl.BlockSpec((tm, d), row)]
        + [pl.BlockSpec((tm, a.shape[1]), row) for a in acts]
        + [_resident(w.shape, const) for w in wos]
        + [pl.BlockSpec((1, d), const),
           _resident((d, f), const),
           _resident((d, f), lambda i: (0, 1)),
           _resident((f, d), const),
           pl.BlockSpec((1, d), const)],
        out_specs=pl.BlockSpec((tm, d), row),
        out_shape=jax.ShapeDtypeStruct((n, d), F32),
        compiler_params=_params(1),
        name="ffn_final" if final else ("ffn_mix" if mix else "ffn"),
    )(x, *acts, *wos, g.reshape(1, d), w_gu, w_gu, w_down, gf.reshape(1, d))


def _norm_matmul_kernel(x_ref, g_ref, w_ref, o_ref):
    xn = _rms(x_ref[...], g_ref[...]).astype(BF16)
    o_ref[...] = jnp.dot(xn, w_ref[...], preferred_element_type=F32)


def _norm_matmul(x, g, w):
    n, d = x.shape
    dout = w.shape[1]
    tm = min(TOKEN_TILE, n)
    const = lambda i: (0, 0)
    return pl.pallas_call(
        _norm_matmul_kernel,
        grid=(n // tm,),
        in_specs=[pl.BlockSpec((tm, d), lambda i: (i, 0)),
                  pl.BlockSpec((1, d), const),
                  _resident((d, dout), const)],
        out_specs=pl.BlockSpec((tm, dout), lambda i: (i, 0)),
        out_shape=jax.ShapeDtypeStruct((n, dout), F32),
        compiler_params=_params(1),
        name="norm_matmul",
    )(x, g.reshape(1, d), w)


def _causal_conv(x, cw, cb, pad_ref):
    s = x.shape[0]
    pad_ref[0:SUBLANES, :] = jnp.zeros((SUBLANES, x.shape[1]), F32)
    pad_ref[SUBLANES:SUBLANES + s, :] = x
    y = cb + cw[CONV_WIDTH - 1:CONV_WIDTH, :] * x
    for j in range(CONV_WIDTH - 1):
        off = SUBLANES - (CONV_WIDTH - 1 - j)
        y = y + cw[j:j + 1, :] * pad_ref[off:off + s, :]
    return y


def _causal_conv_blocks(x, cw, cb, pad_ref, emit, block=256):
    s = x.shape[0]
    pad_ref[0:SUBLANES, :] = jnp.zeros((SUBLANES, x.shape[1]), F32)
    pad_ref[SUBLANES:SUBLANES + s, :] = x
    for r0 in range(0, s, block):
        y = cb + cw[CONV_WIDTH - 1:CONV_WIDTH, :] * pad_ref[SUBLANES + r0:SUBLANES + r0 + block, :]
        for j in range(CONV_WIDTH - 1):
            off = r0 + SUBLANES - (CONV_WIDTH - 1 - j)
            y = y + cw[j:j + 1, :] * pad_ref[off:off + block, :]
        emit(r0, y)


def _rglru_kernel(xa_ref, ga_ref, cw_ref, cb_ref, wr_ref, br_ref, wi_ref, bi_ref, lam_ref,
                  o_ref, pad_ref, a_ref, u_ref):
    s = xa_ref.shape[1]
    ng = s // SUBLANES
    xc = _causal_conv(xa_ref[0], cw_ref[...], cb_ref[...], pad_ref)
    xb = xc.astype(BF16)
    r = jax.nn.sigmoid(jnp.dot(xb, wr_ref[...], preferred_element_type=F32) + br_ref[...])
    ig = jax.nn.sigmoid(jnp.dot(xb, wi_ref[...], preferred_element_type=F32) + bi_ref[...])
    log_a = -RG_C * r * _softplus(-lam_ref[...])
    a = jnp.exp(log_a)
    th = jnp.tanh(log_a)
    u = jnp.sqrt(-2.0 * th / (1.0 - th)) * (ig * xc)
    a3 = a.reshape(ng, SUBLANES, LANES)
    u3 = u.reshape(ng, SUBLANES, LANES)
    row = lax.broadcasted_iota(jnp.int32, (ng, SUBLANES, LANES), 1)
    sh = 1
    while sh < SUBLANES:
        a_s = pltpu.roll(a3, sh, 1)
        u_s = pltpu.roll(u3, sh, 1)
        m = row >= sh
        u3 = jnp.where(m, a3 * u_s + u3, u3)
        a3 = jnp.where(m, a3 * a_s, a3)
        sh *= 2
    a_ref[...] = a3
    u_ref[...] = u3

    def body(i, h):
        hh = a_ref[i] * h + u_ref[i]
        u_ref[i] = hh
        return hh[SUBLANES - 1:SUBLANES, :]

    lax.fori_loop(0, ng, body, jnp.zeros((1, LANES), F32), unroll=8)
    o_ref[0] = (_gelu_tanh(ga_ref[0]) * u_ref[...].reshape(s, LANES)).astype(o_ref.dtype)


def _rglru(proj3, cw, cb, wr_bd, br, wi_bd, bi, lam):
    bsz, s, _ = proj3.shape
    c = cw.shape[1]
    nb = c // LANES
    vec = lambda b, j: (0, j)
    return pl.pallas_call(
        _rglru_kernel,
        grid=(bsz, nb),
        in_specs=[pl.BlockSpec((1, s, LANES), lambda b, j: (b, 0, j)),
                  pl.BlockSpec((1, s, LANES), lambda b, j: (b, 0, nb + j)),
                  pl.BlockSpec((CONV_WIDTH, LANES), vec),
                  pl.BlockSpec((1, LANES), vec),
                  pl.BlockSpec((LANES, LANES), lambda b, j: (j, j)),
                  pl.BlockSpec((1, LANES), vec),
                  pl.BlockSpec((LANES, LANES), lambda b, j: (j, j)),
                  pl.BlockSpec((1, LANES), vec),
                  pl.BlockSpec((1, LANES), vec)],
        out_specs=pl.BlockSpec((1, s, LANES), lambda b, j: (b, 0, j)),
        out_shape=jax.ShapeDtypeStruct((bsz, s, c), BF16),
        scratch_shapes=[pltpu.VMEM((s + SUBLANES, LANES), F32),
                        pltpu.VMEM((s // SUBLANES, SUBLANES, LANES), F32),
                        pltpu.VMEM((s // SUBLANES, SUBLANES, LANES), F32)],
        compiler_params=_params(2),
        name="rglru",
    )(proj3, proj3, cw, cb.reshape(1, c), wr_bd, br.reshape(1, c), wi_bd, bi.reshape(1, c),
      lam.reshape(1, c))


def _mlstm_kernel(bias_ref, q_ref, k_ref, v_ref, og_ref, gi_ref, gf_ref, cwq_ref, cbq_ref,
                  cwk_ref, cbk_ref, ng_ref, o_ref, pad_ref, q_sc, k_sc, li_sc, lf_sc, bc_sc):
    hd = pl.program_id(1)
    s = q_ref.shape[1]
    dh = q_ref.shape[2]
    nc = s // ML_CHUNK
    def put_q(r0, y):
        q_sc[r0:r0 + y.shape[0], :] = jax.nn.silu(y) * (dh ** -0.5)

    def put_k(r0, y):
        k_sc[r0:r0 + y.shape[0], :] = jax.nn.silu(y)

    _causal_conv_blocks(q_ref[0], cwq_ref[...], cbq_ref[...], pad_ref, put_q)
    _causal_conv_blocks(k_ref[0], cwk_ref[...], cbk_ref[...], pad_ref, put_k)
    li_sc[...] = gi_ref[0, 0] + bias_ref[hd]
    lf = -_softplus(-(gf_ref[0, 0] + bias_ref[ML_HEADS + hd]))
    lf_sc[...] = lf
    jj = lax.broadcasted_iota(jnp.int32, (ML_CHUNK, ML_CHUNK), 0)
    kk = lax.broadcasted_iota(jnp.int32, (ML_CHUNK, ML_CHUNK), 1)
    tri = kk <= jj
    eye = kk == jj
    upper = jnp.where(jj <= kk, 1.0, 0.0).astype(F32)
    bc_sc[...] = jnp.dot(lf, upper, precision=HIGHEST, preferred_element_type=F32)
    norm_g = ng_ref[...]

    def chunk(c, carry):
        c_st, n_st, m_st = carry
        r0 = pl.multiple_of(c * ML_CHUNK, ML_CHUNK)
        qc = q_sc[pl.ds(r0, ML_CHUNK), :]
        kc = k_sc[pl.ds(r0, ML_CHUNK), :]
        vc = v_ref[0, pl.ds(r0, ML_CHUNK), :]
        li_row = li_sc[pl.ds(c, 1), :]
        lf_row = lf_sc[pl.ds(c, 1), :]
        b_row = bc_sc[pl.ds(c, 1), :]
        b_col = jnp.sum(jnp.where(tri, lf_row, 0.0), axis=1, keepdims=True)
        li_col = jnp.sum(jnp.where(eye, li_row, 0.0), axis=1, keepdims=True)
        g_tot = jnp.sum(lf_row, axis=1, keepdims=True)
        w_col = g_tot - b_col + li_col
        m_loc = jnp.max(w_col, axis=0, keepdims=True)
        kw = kc * jnp.exp(w_col - m_loc)
        c_loc = jnp.dot(vc.T.astype(BF16), kw.astype(BF16), preferred_element_type=F32)
        n_loc = jnp.sum(kw, axis=0, keepdims=True)
        d = jnp.where(tri, b_col - b_row + li_row, NEG)
        m_inter = b_col + m_st
        m = jnp.maximum(m_inter, jnp.max(d, axis=1, keepdims=True))
        qb = qc.astype(BF16)
        qk = lax.dot_general(qb, kc.astype(BF16), NT_DIMS, preferred_element_type=F32)
        p = jnp.exp(d - m) * qk
        sc = jnp.exp(m_inter - m)
        inter = lax.dot_general(qb, c_st.astype(BF16), NT_DIMS, preferred_element_type=F32)
        num = sc * inter + jnp.dot(p.astype(BF16), vc.astype(BF16), preferred_element_type=F32)
        den = sc * jnp.sum(qc * n_st, axis=1, keepdims=True) + jnp.sum(p, axis=1, keepdims=True)
        h = num / jnp.maximum(jnp.abs(den), jnp.exp(-m))
        h = h * lax.rsqrt(jnp.mean(h * h, axis=1, keepdims=True) + EPS) * norm_g
        o_ref[0, pl.ds(r0, ML_CHUNK), :] = (
            jax.nn.sigmoid(og_ref[0, pl.ds(r0, ML_CHUNK), :]) * h).astype(o_ref.dtype)
        m_new = jnp.maximum(g_tot + m_st, m_loc)
        sa = jnp.exp(g_tot + m_st - m_new)
        sb = jnp.exp(m_loc - m_new)
        return sa * c_st + sb * c_loc, sa * n_st + sb * n_loc, m_new

    init = (jnp.zeros((dh, dh), F32), jnp.zeros((1, dh), F32), jnp.full((1, 1), NEG, F32))
    lax.fori_loop(0, nc, chunk, init, unroll=2)


def _mlstm(proj3, gates_t, bias, cw, cb, norm_g, *, col0):
    bsz, s, _ = proj3.shape
    nh = ML_HEADS
    dh = LANES
    nc = s // ML_CHUNK
    base = col0 // dh
    blk = lambda off: pl.BlockSpec((1, s, dh), lambda b, h: (b, 0, base + off + h))
    vec = lambda off: (lambda b, h: (0, off + h))
    return pl.pallas_call(
        _mlstm_kernel,
        grid=(bsz, nh),
        in_specs=[pl.BlockSpec(memory_space=pltpu.SMEM),
                  blk(0), blk(nh), blk(2 * nh), blk(3 * nh),
                  pl.BlockSpec((1, 1, nc, ML_CHUNK), lambda b, h: (b, h, 0, 0)),
                  pl.BlockSpec((1, 1, nc, ML_CHUNK), lambda b, h: (b, nh + h, 0, 0)),
                  pl.BlockSpec((CONV_WIDTH, dh), vec(0)), pl.BlockSpec((1, dh), vec(0)),
                  pl.BlockSpec((CONV_WIDTH, dh), vec(nh)), pl.BlockSpec((1, dh), vec(nh)),
                  pl.BlockSpec((1, dh), vec(0))],
        out_specs=pl.BlockSpec((1, s, dh), lambda b, h: (b, 0, h)),
        out_shape=jax.ShapeDtypeStruct((bsz, s, nh * dh), BF16),
        scratch_shapes=[pltpu.VMEM((s + SUBLANES, dh), F32),
                        pltpu.VMEM((s, dh), F32), pltpu.VMEM((s, dh), F32),
                        pltpu.VMEM((nc, ML_CHUNK), F32), pltpu.VMEM((nc, ML_CHUNK), F32),
                        pltpu.VMEM((nc, ML_CHUNK), F32)],
        compiler_params=_params(2),
        name="mlstm",
    )(bias, proj3, proj3, proj3, proj3, gates_t, gates_t, cw, cb.reshape(1, -1), cw,
      cb.reshape(1, -1), norm_g.reshape(1, -1))


def _ab_mixer(x, g, w_in, rg_conv_w, rg_conv_b, rg_w_r, rg_b_r, rg_w_i, rg_b_i, rg_lambda,
              ml_conv_w, ml_conv_b, ml_b_i, ml_b_f, ml_norm, w_out, bsz, s):
    d_rg = rg_conv_w.shape[1]
    d_ml = ml_norm.shape[0]
    d_main = 2 * d_rg + 4 * d_ml
    n_gate = 2 * ML_HEADS
    w_in_p = jnp.pad(w_in, ((0, 0), (0, LANES - n_gate))).astype(BF16)
    proj = _norm_matmul(x, g, w_in_p)
    proj3 = proj.reshape(bsz, s, proj.shape[1])
    bd = lambda w: jax.scipy.linalg.block_diag(*[w[i] for i in range(RG_BLOCKS)]).astype(BF16)
    ya = _rglru(proj3, rg_conv_w, rg_conv_b, bd(rg_w_r), rg_b_r, bd(rg_w_i), rg_b_i, rg_lambda)
    gates_t = proj3[:, :, d_main:d_main + n_gate].transpose(0, 2, 1).reshape(
        bsz, n_gate, s // ML_CHUNK, ML_CHUNK)
    yb = _mlstm(proj3, gates_t, jnp.concatenate([ml_b_i, ml_b_f]), ml_conv_w, ml_conv_b, ml_norm,
                col0=2 * d_rg)
    w_out_b = w_out.astype(BF16)
    return [(ya.reshape(bsz * s, d_rg), w_out_b[:d_rg]), (yb.reshape(bsz * s, d_ml), w_out_b[d_rg:])]


def _nsa_proj_kernel(x_ref, g_ref, w_ref, q_ref, gt_ref, kc_ref, vc_ref, ks_ref, kw_ref, vs_ref, vw_ref,
                     *, seq):
    tm = x_ref.shape[0]
    kv = NSA_GROUPS * NSA_HEAD_DIM
    width = NSA_HEADS * NSA_HEAD_DIM
    half = LANES // 2
    xn = _rms(x_ref[...], g_ref[...]).astype(BF16)
    proj = jnp.dot(xn, w_ref[...], preferred_element_type=F32)
    lane = lax.broadcasted_iota(jnp.int32, (tm, LANES), 1)
    lo = lane < half

    def lane_pair(c0):
        p = proj[:, c0:c0 + LANES]
        return p, pltpu.roll(p, half, 1)

    scale = NSA_HEAD_DIM ** -0.5 * LOG2E
    for pair in range(NSA_HEADS // 2):
        for j, piece in enumerate(lane_pair(pair * LANES)):
            h = 2 * pair + j
            q_ref[:, h * LANES:(h + 1) * LANES] = jnp.where(lo, piece * scale, 0.0).astype(BF16)
    gt_ref[0] = proj[:, width + 6 * kv:width + 6 * kv + LANES].T
    kc_ref[...] = proj[:, width:width + kv]
    vc_ref[...] = proj[:, width + kv:width + 2 * kv]
    pos = (pl.program_id(0) % (seq // tm)) * tm + lax.broadcasted_iota(jnp.int32, (tm, LANES), 0)
    onehot = jnp.where(lane == half + pos // SEL_BLOCK, 1.0, 0.0)
    for pair in range(NSA_GROUPS // 2):
        for idx, ref, fill in ((2, ks_ref, onehot), (4, kw_ref, 0.0)):
            for j, piece in enumerate(lane_pair(width + idx * kv + pair * LANES)):
                ref[0, 2 * pair + j] = jnp.where(lo, piece, fill).astype(BF16)
        for idx, ref in ((3, vs_ref), (5, vw_ref)):
            c0 = width + idx * kv + pair * LANES
            t = proj[:, c0:c0 + LANES].T.astype(BF16)
            ones = jnp.ones((NSA_VROWS - half, NSA_QTILE), BF16)
            for j in range(tm // NSA_QTILE):
                for hf in range(2):
                    ref[0, 2 * pair + hf, j, 0:half, :] = t[hf * half:(hf + 1) * half,
                                                            j * NSA_QTILE:(j + 1) * NSA_QTILE]
                    ref[0, 2 * pair + hf, j, half:NSA_VROWS, :] = ones


def _nsa_proj(x, g, w, bsz, s):
    n, d = x.shape
    dout = w.shape[1]
    tm = min(TOKEN_TILE, s)
    nsb = s // tm
    ng, dk = NSA_GROUPS, NSA_HEAD_DIM
    kv = ng * dk
    tq = NSA_QTILE
    const = lambda i: (0, 0)
    row = lambda i: (i, 0)
    keyed = lambda i: (i // nsb, 0, i % nsb, 0)
    return pl.pallas_call(
        functools.partial(_nsa_proj_kernel, seq=s),
        grid=(n // tm,),
        in_specs=[pl.BlockSpec((tm, d), row), pl.BlockSpec((1, d), const), _resident((d, dout), const)],
        out_specs=[pl.BlockSpec((tm, NSA_HEADS * LANES), row),
                   pl.BlockSpec((1, LANES, tm), lambda i: (i // nsb, 0, i % nsb)),
                   pl.BlockSpec((tm, kv), row), pl.BlockSpec((tm, kv), row),
                   pl.BlockSpec((1, ng, tm, LANES), keyed), pl.BlockSpec((1, ng, tm, LANES), keyed),
                   pl.BlockSpec((1, ng, tm // tq, NSA_VROWS, tq), lambda i: (i // nsb, 0, i % nsb, 0, 0)),
                   pl.BlockSpec((1, ng, tm // tq, NSA_VROWS, tq), lambda i: (i // nsb, 0, i % nsb, 0, 0))],
        out_shape=[jax.ShapeDtypeStruct((n, NSA_HEADS * LANES), BF16),
                   jax.ShapeDtypeStruct((bsz, LANES, s), F32),
                   jax.ShapeDtypeStruct((n, kv), F32), jax.ShapeDtypeStruct((n, kv), F32),
                   jax.ShapeDtypeStruct((bsz, ng, s, LANES), BF16),
                   jax.ShapeDtypeStruct((bsz, ng, s, LANES), BF16),
                   jax.ShapeDtypeStruct((bsz, ng, s // tq, NSA_VROWS, tq), BF16),
                   jax.ShapeDtypeStruct((bsz, ng, s // tq, NSA_VROWS, tq), BF16)],
        compiler_params=_params(1),
        name="nsa_proj",
    )(x, g.reshape(1, d), w)


def _compress_kernel(x_ref, pe_ref, w1_ref, b1_ref, w2_ref, o_ref, *, transpose_out):
    nb = x_ref.shape[1] // CMP_STRIDE
    top = bot = None
    for l in range(CMP_STRIDE):
        rows = x_ref[0, pl.ds(l, nb, stride=CMP_STRIDE), :]
        t = jnp.dot((rows + pe_ref[l:l + 1, :]).astype(BF16), w1_ref[l], preferred_element_type=F32)
        b = jnp.dot((rows + pe_ref[CMP_STRIDE + l:CMP_STRIDE + l + 1, :]).astype(BF16),
                    w1_ref[CMP_STRIDE + l], preferred_element_type=F32)
        top = t if top is None else top + t
        bot = b if bot is None else bot + b
    pre = top + pltpu.roll(bot, nb - 1, 0) + b1_ref[...]
    out = jnp.dot(_gelu_tanh(pre).astype(BF16), w2_ref[...], preferred_element_type=F32)
    out = jnp.where(lax.broadcasted_iota(jnp.int32, out.shape, 0) < nb - 1, out, 0.0)
    for hf in range(2):
        piece = out[:, hf * LANES:(hf + 1) * LANES]
        if transpose_out:
            o_ref[0, hf] = piece.T[0:NSA_HEAD_DIM, :].astype(o_ref.dtype)
        else:
            o_ref[0, hf] = piece.astype(o_ref.dtype)


def _compress(kv_in, pe, w1, b1, w2, *, transpose_out):
    bsz, s, _ = kv_in.shape
    dk, hid = NSA_HEAD_DIM, w1.shape[1]
    nb = s // CMP_STRIDE
    w1r = w1.reshape(CMP_BLOCK, dk, hid)
    z = jnp.zeros_like(w1r)
    w1p = jnp.concatenate([jnp.concatenate([w1r, z], axis=2), jnp.concatenate([z, w1r], axis=2)],
                          axis=1).astype(BF16)
    w2w = jnp.pad(w2, ((0, 0), (0, LANES - dk)))
    z2 = jnp.zeros_like(w2w)
    w2p = jnp.concatenate([jnp.concatenate([w2w, z2], axis=1), jnp.concatenate([z2, w2w], axis=1)],
                          axis=0).astype(BF16)
    const2 = lambda b, p: (0, 0)
    out_block = (1, 2, dk, nb) if transpose_out else (1, 2, nb, LANES)
    out_shape = (bsz, NSA_GROUPS, dk, nb) if transpose_out else (bsz, NSA_GROUPS, nb, LANES)
    return pl.pallas_call(
        functools.partial(_compress_kernel, transpose_out=transpose_out),
        grid=(bsz, NSA_GROUPS // 2),
        in_specs=[pl.BlockSpec((1, s, LANES), lambda b, p: (b, 0, p)),
                  pl.BlockSpec((CMP_BLOCK, LANES), const2),
                  _resident((CMP_BLOCK, LANES, 2 * hid), lambda b, p: (0, 0, 0)),
                  pl.BlockSpec((1, 2 * hid), const2),
                  _resident((2 * hid, 2 * LANES), const2)],
        out_specs=pl.BlockSpec(out_block, lambda b, p: (b, p, 0, 0)),
        out_shape=jax.ShapeDtypeStruct(out_shape, BF16),
        compiler_params=_params(2),
        name="nsa_compress",
    )(kv_in, jnp.concatenate([pe, pe], axis=1), w1p, jnp.concatenate([b1, b1]).reshape(1, 2 * hid), w2p)


def _nsa_kernel(bg_ref, q_ref, gt_ref, kc_ref, vc_ref, ks_ref, vs_ref, kw_ref, vw_ref, msel_ref,
                o_ref, qa_sc, s_sc, m_sc, acc_sc):
    grp = pl.program_id(1)
    qi = pl.program_id(2)
    tq = q_ref.shape[0]
    rep = NSA_HEADS // NSA_GROUPS
    dk = NSA_HEAD_DIM
    half = LANES // 2
    s0 = qi * tq
    key_off = lax.broadcasted_iota(jnp.int32, (tq, tq), 0)
    t_lane = s0 + lax.broadcasted_iota(jnp.int32, (tq, tq), 1)
    head = lambda r: slice(r * LANES, (r + 1) * LANES)
    cols = lambda r: slice(r * tq, (r + 1) * tq)

    n_blk = kc_ref.shape[2]
    blk_id = lax.broadcasted_iota(jnp.int32, (n_blk, tq), 0)
    t_blk = s0 + lax.broadcasted_iota(jnp.int32, (n_blk, tq), 1)
    cmask = (blk_id * CMP_STRIDE + (CMP_BLOCK - 1) <= t_blk) & (blk_id < n_blk - 1)
    kc = kc_ref[0, 0]
    vc_t = vc_ref[0, 0]
    psum = None
    o_cmp = []
    cmp_scores = [lax.dot_general(kc, q_ref[:, head(r)], NT_DIMS, preferred_element_type=F32)
                  for r in range(rep)]
    for r in range(rep):
        scm = jnp.where(cmask, cmp_scores[r], NEG)
        e = jnp.where(cmask, jnp.exp2(scm - jnp.max(scm, axis=0, keepdims=True)), 0.0)
        den = jnp.sum(e, axis=0, keepdims=True)
        p = e * (1.0 / jnp.where(den > 0.0, den, 1.0))
        psum = p if psum is None else psum + p
        o_cmp.append(jnp.dot(vc_t, p.astype(BF16), preferred_element_type=F32))

    score_t = jnp.dot(msel_ref[...], psum, precision=HIGHEST, preferred_element_type=F32)
    n_sel = 32
    st = score_t[half:half + n_sel, :]
    jb = lax.broadcasted_iota(jnp.int32, (n_sel, tq), 0)
    cur = (s0 + lax.broadcasted_iota(jnp.int32, (n_sel, tq), 1)) // SEL_BLOCK
    valid = jb <= cur
    forced = (jb == 0) | (jb == cur) | (jb == cur - 1)
    s2 = jnp.where(forced & valid, FORCED_SCORE, jnp.where(valid, st, -1.0))
    rank = jnp.zeros((n_sel, tq), jnp.int32)
    for i in range(n_sel):
        si = s2[i:i + 1, :]
        beats = (si > s2) | ((si == s2) & (jb > i))
        rank = rank + jnp.where(beats, 1, 0)
    ind = jnp.where((rank < SEL_TOPN) & valid, 1.0, 0.0)
    ind_full = jnp.concatenate([jnp.zeros((half, tq), F32), ind, jnp.zeros((LANES - half - n_sel, tq), F32)],
                               axis=0)
    lane_q = lax.broadcasted_iota(jnp.int32, (tq, LANES), 1)
    bias = jnp.where((lane_q >= half) & (lane_q < half + n_sel), (ind_full.T - 1.0) * (-NEG), 0.0)
    for r in range(rep):
        qa_sc[r] = (q_ref[:, head(r)].astype(F32) + bias).astype(BF16)

    m_sc[...] = jnp.full(m_sc.shape, NEG, F32)
    acc_sc[...] = jnp.zeros(acc_sc.shape, F32)

    def qk_scores(kt):
        k_tile = ks_ref[0, 0, pl.ds(pl.multiple_of(kt * tq, tq), tq), :]
        return [lax.dot_general(k_tile, qa_sc[r], NT_DIMS, preferred_element_type=F32)
                for r in range(rep)]

    def qk_into(kt, slot):
        for r, s in enumerate(qk_scores(kt)):
            s_sc[slot, r] = s

    def softmax_pv(kt, scores, diagonal):
        v_t = vs_ref[0, 0, kt]
        for r in range(rep):
            s = scores(r)
            if diagonal:
                s = jnp.where(kt * tq + key_off <= t_lane, s, NEG)
            m_old = m_sc[:, cols(r)]
            m_new = jnp.maximum(m_old, jnp.max(s, axis=0, keepdims=True))
            alpha = jnp.exp2(m_old - m_new)
            p = jnp.exp2(s - m_new)
            acc_sc[:, cols(r)] = alpha * acc_sc[:, cols(r)] + jnp.dot(
                v_t, p.astype(BF16), preferred_element_type=F32)
            m_sc[:, cols(r)] = m_new

    from_slot = lambda slot: (lambda r: s_sc[slot, r])

    qk_into(0, 0)

    def pair_body(j, carry):
        a = 2 * j
        qk_into(a + 1, 1)
        softmax_pv(a, from_slot(0), False)
        qk_into(a + 2, 0)
        softmax_pv(a + 1, from_slot(1), False)
        return carry

    lax.fori_loop(0, qi // 2, pair_body, 0)

    @pl.when(qi % 2 == 1)
    def _():
        last = qk_scores(qi)
        softmax_pv(qi - 1, from_slot(0), False)
        softmax_pv(qi, lambda r: last[r], True)

    @pl.when(qi % 2 == 0)
    def _():
        softmax_pv(qi, from_slot(0), True)

    kp = jnp.maximum(qi - 1, 0)
    p0 = pl.multiple_of(kp * tq, tq)
    d0 = pl.multiple_of(qi * tq, tq)
    k_prev = kw_ref[0, 0, pl.ds(p0, tq), :]
    k_diag = kw_ref[0, 0, pl.ds(d0, tq), :]
    v_prev = vw_ref[0, 0, kp]
    v_diag = vw_ref[0, 0, qi]
    prev_mask = (p0 + key_off > t_lane - WINDOW) & (qi >= 1)
    diag_mask = d0 + key_off <= t_lane
    merged = []
    win_scores = [(lax.dot_general(k_prev, q_ref[:, head(r)], NT_DIMS, preferred_element_type=F32),
                   lax.dot_general(k_diag, q_ref[:, head(r)], NT_DIMS, preferred_element_type=F32))
                  for r in range(rep)]
    for r in range(rep):
        s_prev = jnp.where(prev_mask, win_scores[r][0], NEG)
        s_diag = jnp.where(diag_mask, win_scores[r][1], NEG)
        mw = jnp.maximum(jnp.max(s_prev, axis=0, keepdims=True), jnp.max(s_diag, axis=0, keepdims=True))
        p_prev = jnp.exp2(s_prev - mw)
        p_diag = jnp.exp2(s_diag - mw)
        win = (jnp.dot(v_prev, p_prev.astype(BF16), preferred_element_type=F32)
               + jnp.dot(v_diag, p_diag.astype(BF16), preferred_element_type=F32))
        o_win = win[0:dk] * (1.0 / win[dk:dk + 1])
        o_slc = acc_sc[0:dk, cols(r)] * (1.0 / acc_sc[dk:dk + 1, cols(r)])
        tot = None
        for br, o_br in enumerate((o_cmp[r], o_slc, o_win)):
            idx = grp * (rep * NSA_BRANCHES) + r * NSA_BRANCHES + br
            gate = jax.nn.sigmoid(gt_ref[0, pl.ds(idx, 1), :] + bg_ref[idx])
            tot = gate * o_br if tot is None else tot + gate * o_br
        merged.append(tot)
    for pair in range(rep // 2):
        both = jnp.concatenate([merged[2 * pair], merged[2 * pair + 1]], axis=0)
        o_ref[:, pair * LANES:(pair + 1) * LANES] = both.T.astype(o_ref.dtype)


def _nsa_attention(q, gt, b_gate, kc, vc_t, ks, vs_t, kw, vw_t, msel_t, bsz, s):
    n = q.shape[0]
    tq = NSA_QTILE
    assert tq == WINDOW and s % tq == 0 and s // SEL_BLOCK == 32
    rep = NSA_HEADS // NSA_GROUPS
    dk = NSA_HEAD_DIM
    nq = s // tq
    per_bg = lambda a: pl.BlockSpec((1, 1) + a.shape[2:], lambda b, g, i: (b, g) + (0,) * (a.ndim - 2))
    return pl.pallas_call(
        _nsa_kernel,
        grid=(bsz, NSA_GROUPS, nq),
        in_specs=[pl.BlockSpec(memory_space=pltpu.SMEM),
                  pl.BlockSpec((tq, rep * LANES), lambda b, g, i: (b * nq + i, g)),
                  pl.BlockSpec((1, LANES, tq), lambda b, g, i: (b, 0, i)),
                  per_bg(kc), per_bg(vc_t), per_bg(ks), per_bg(vs_t), per_bg(kw), per_bg(vw_t),
                  pl.BlockSpec(msel_t.shape, lambda b, g, i: (0, 0))],
        out_specs=pl.BlockSpec((tq, rep * dk), lambda b, g, i: (b * nq + i, g)),
        out_shape=jax.ShapeDtypeStruct((n, NSA_HEADS * dk), BF16),
        scratch_shapes=[pltpu.VMEM((rep, tq, LANES), BF16),
                        pltpu.VMEM((2, rep, tq, tq), F32),
                        pltpu.VMEM((1, rep * tq), F32),
                        pltpu.VMEM((NSA_VROWS, rep * tq), F32)],
        compiler_params=_params(3),
        name="nsa_attention",
    )(b_gate, q, gt, kc, vc_t, ks, vs_t, kw, vw_t, msel_t)


def _sel_matrix_t(n_rows, n_sel):
    n_cmp = np.arange(n_rows)[None, :] * CMP_STRIDE
    sel = np.arange(n_sel)[:, None] * SEL_BLOCK
    ov = np.clip(np.minimum(n_cmp + CMP_BLOCK, sel + SEL_BLOCK) - np.maximum(n_cmp, sel), 0, None)
    out = np.zeros((LANES, n_rows), np.float32)
    out[LANES // 2:LANES // 2 + n_sel] = ov / CMP_BLOCK
    return jnp.asarray(out)


def _nsa_mixer(x, g, w_in, pe_k, k_w1, k_b1, k_w2, pe_v, v_w1, v_b1, v_w2, b_gate, w_out, bsz, s):
    n_gate = NSA_BRANCHES * NSA_HEADS
    kv = NSA_GROUPS * NSA_HEAD_DIM
    w_in_p = jnp.pad(w_in, ((0, 0), (0, LANES - n_gate))).astype(BF16)
    q, gt, kc_in, vc_in, ks, kw, vs_t, vw_t = _nsa_proj(x, g, w_in_p, bsz, s)
    kc = _compress(kc_in.reshape(bsz, s, kv), pe_k, k_w1, k_b1, k_w2, transpose_out=False)
    vc_t = _compress(vc_in.reshape(bsz, s, kv), pe_v, v_w1, v_b1, v_w2, transpose_out=True)
    msel_t = _sel_matrix_t(s // CMP_STRIDE, s // SEL_BLOCK)
    attn = _nsa_attention(q, gt, b_gate, kc, vc_t, ks, vs_t, kw, vw_t, msel_t, bsz, s)
    return [(attn, w_out.astype(BF16))]


def kernel(x, ffn1_norm, ffn1_w_gu, ffn1_w_down, mix_norm, ffn2_norm, ffn2_w_gu, ffn2_w_down,
           ab_w_in, rg_conv_w, rg_conv_b, rg_w_r, rg_b_r, rg_w_i, rg_b_i, rg_lambda,
           ml_conv_w, ml_conv_b, ml_b_i, ml_b_f, ml_norm, ab_w_out,
           nsa_w_in, nsa_pe_k, nsa_k_w1, nsa_k_b1, nsa_k_w2, nsa_pe_v, nsa_v_w1, nsa_v_b1, nsa_v_w2,
           nsa_b_gate, nsa_w_out, final_norm):
    bsz, s, d = x.shape
    depth = ffn1_norm.shape[0]
    h = x.reshape(bsz * s, d)
    for i in range(depth):
        j = i // 2
        h = _ffn(h, [], ffn1_norm[i], ffn1_w_gu[i].astype(BF16), ffn1_w_down[i].astype(BF16),
                 final_norm, final=False)
        if i % 2 == 0:
            mix = _ab_mixer(h, mix_norm[i], ab_w_in[j], rg_conv_w[j], rg_conv_b[j], rg_w_r[j], rg_b_r[j],
                            rg_w_i[j], rg_b_i[j], rg_lambda[j], ml_conv_w[j], ml_conv_b[j], ml_b_i[j],
                            ml_b_f[j], ml_norm[j], ab_w_out[j], bsz, s)
        else:
            mix = _nsa_mixer(h, mix_norm[i], nsa_w_in[j], nsa_pe_k[j], nsa_k_w1[j], nsa_k_b1[j],
                             nsa_k_w2[j], nsa_pe_v[j], nsa_v_w1[j], nsa_v_b1[j], nsa_v_w2[j],
                             nsa_b_gate[j], nsa_w_out[j], bsz, s)
        h = _ffn(h, mix, ffn2_norm[i], ffn2_w_gu[i].astype(BF16), ffn2_w_down[i].astype(BF16),
                 final_norm, final=(i == depth - 1))
    return h.reshape(bsz, s, d)
```

```python
import functools

import jax
import jax.numpy as jnp
import numpy as np
from jax import lax
from jax.experimental import pallas as pl
from jax.experimental.pallas import tpu as pltpu

F32 = jnp.float32
BF16 = jnp.bfloat16
HIGHEST = lax.Precision.HIGHEST

EPS = 1e-6
RG_C = 8.0
RG_BLOCKS = 8
CONV_WIDTH = 4
ML_HEADS = 4
ML_CHUNK = 128
NSA_HEADS = 16
NSA_GROUPS = 4
NSA_HEAD_DIM = 64
NSA_BRANCHES = 3
CMP_BLOCK = 32
CMP_STRIDE = 16
SEL_BLOCK = 64
SEL_TOPN = 8
WINDOW = 256
FORCED_SCORE = 1e6

LANES = 128
SUBLANES = 8
NEG = -1e30
TAKEN = -2.0
VMEM_LIMIT = 48 * 1024 * 1024

TOKEN_TILE = 512
NSA_QTILE = 256
NSA_VROWS = 80
NSA_STEP_GROUPS = 2
LOG2E = 1.4426950408889634
NT_DIMS = (((1,), (1,)), ((), ()))


def _params(n_axes):
    return pltpu.CompilerParams(dimension_semantics=("arbitrary",) * n_axes,
                                vmem_limit_bytes=VMEM_LIMIT)


def _resident(shape, index_map):
    return pl.BlockSpec(shape, index_map, pipeline_mode=pl.Buffered(1))


def _rms(x, g):
    return x * lax.rsqrt(jnp.mean(x * x, axis=-1, keepdims=True) + EPS) * g


def _gelu_tanh(x):
    return 0.5 * x * (1.0 + jnp.tanh(0.7978845608028654 * (x + 0.044715 * (x * x * x))))


def _softplus(z):
    return jnp.maximum(z, 0.0) + jnp.log1p(jnp.exp(-jnp.abs(z)))


def _ffn_kernel(*refs, n_mix, n_chunks, final):
    x_ref = refs[0]
    a_refs = refs[1:1 + n_mix]
    wo_refs = refs[1 + n_mix:1 + 2 * n_mix]
    g_ref, wg_ref, wu_ref, wd_ref, gf_ref, o_ref = refs[1 + 2 * n_mix:]
    x = x_ref[...]
    for a_ref, wo_ref in zip(a_refs, wo_refs):
        x = x + jnp.dot(a_ref[...], wo_ref[...], preferred_element_type=F32)
    xn = _rms(x, g_ref[...]).astype(BF16)
    fc = wg_ref.shape[1] // n_chunks
    acc = None
    for c in range(n_chunks):
        gate = jnp.dot(xn, wg_ref[:, c * fc:(c + 1) * fc], preferred_element_type=F32)
        up = jnp.dot(xn, wu_ref[:, c * fc:(c + 1) * fc], preferred_element_type=F32)
        h = (gate * jax.nn.sigmoid(gate) * up).astype(BF16)
        part = jnp.dot(h, wd_ref[c * fc:(c + 1) * fc, :], preferred_element_type=F32)
        acc = part if acc is None else acc + part
    y = x + 0.5 * acc
    if final:
        y = _rms(y, gf_ref[...])
    o_ref[...] = y


def _ffn(x, mix, g, w_gu, w_down, gf, *, final):
    n, d = x.shape
    f = w_down.shape[0]
    tm = min(TOKEN_TILE, n)
    const = lambda i: (0, 0)
    row = lambda i: (i, 0)
    acts = [a for a, _ in mix]
    wos = [w for _, w in mix]
    return pl.pallas_call(
        functools.partial(_ffn_kernel, n_mix=len(mix), n_chunks=2, final=final),
        grid=(n // tm,),
        in_specs=[pl.BlockSpec((tm, d), row)]
        + [pl.BlockSpec((tm, a.shape[1]), row) for a in acts]
        + [_resident(w.shape, const) for w in wos]
        + [pl.BlockSpec((1, d), const),
           _resident((d, f), const),
           _resident((d, f), lambda i: (0, 1)),
           _resident((f, d), const),
           pl.BlockSpec((1, d), const)],
        out_specs=pl.BlockSpec((tm, d), row),
        out_shape=jax.ShapeDtypeStruct((n, d), F32),
        compiler_params=_params(1),
        name="ffn_final" if final else ("ffn_mix" if mix else "ffn"),
    )(x, *acts, *wos, g.reshape(1, d), w_gu, w_gu, w_down, gf.reshape(1, d))


def _norm_matmul_kernel(x_ref, g_ref, w_ref, o_ref):
    xn = _rms(x_ref[...], g_ref[...]).astype(BF16)
    o_ref[...] = jnp.dot(xn, w_ref[...], preferred_element_type=F32)


def _norm_matmul(x, g, w):
    n, d = x.shape
    dout = w.shape[1]
    tm = min(TOKEN_TILE, n)
    const = lambda i: (0, 0)
    return pl.pallas_call(
        _norm_matmul_kernel,
        grid=(n // tm,),
        in_specs=[pl.BlockSpec((tm, d), lambda i: (i, 0)),
                  pl.BlockSpec((1, d), const),
                  _resident((d, dout), const)],
        out_specs=pl.BlockSpec((tm, dout), lambda i: (i, 0)),
        out_shape=jax.ShapeDtypeStruct((n, dout), F32),
        compiler_params=_params(1),
        name="norm_matmul",
    )(x, g.reshape(1, d), w)


def _causal_conv(x, cw, cb, pad_ref):
    s = x.shape[0]
    pad_ref[0:SUBLANES, :] = jnp.zeros((SUBLANES, x.shape[1]), F32)
    pad_ref[SUBLANES:SUBLANES + s, :] = x
    y = cb + cw[CONV_WIDTH - 1:CONV_WIDTH, :] * x
    for j in range(CONV_WIDTH - 1):
        off = SUBLANES - (CONV_WIDTH - 1 - j)
        y = y + cw[j:j + 1, :] * pad_ref[off:off + s, :]
    return y


def _causal_conv_blocks(x, cw, cb, pad_ref, emit, block=256):
    s = x.shape[0]
    pad_ref[0:SUBLANES, :] = jnp.zeros((SUBLANES, x.shape[1]), F32)
    pad_ref[SUBLANES:SUBLANES + s, :] = x
    for r0 in range(0, s, block):
        y = cb + cw[CONV_WIDTH - 1:CONV_WIDTH, :] * pad_ref[SUBLANES + r0:SUBLANES + r0 + block, :]
        for j in range(CONV_WIDTH - 1):
            off = r0 + SUBLANES - (CONV_WIDTH - 1 - j)
            y = y + cw[j:j + 1, :] * pad_ref[off:off + block, :]
        emit(r0, y)


def _rglru_kernel(xa_ref, ga_ref, cw_ref, cb_ref, wr_ref, br_ref, wi_ref, bi_ref, lam_ref,
                  o_ref, pad_ref, a_ref, u_ref):
    s = xa_ref.shape[1]
    ng = s // SUBLANES
    xc = _causal_conv(xa_ref[0], cw_ref[...], cb_ref[...], pad_ref)
    xb = xc.astype(BF16)
    r = jax.nn.sigmoid(jnp.dot(xb, wr_ref[...], preferred_element_type=F32) + br_ref[...])
    ig = jax.nn.sigmoid(jnp.dot(xb, wi_ref[...], preferred_element_type=F32) + bi_ref[...])
    log_a = -RG_C * r * _softplus(-lam_ref[...])
    a = jnp.exp(log_a)
    th = jnp.tanh(log_a)
    u = jnp.sqrt(-2.0 * th / (1.0 - th)) * (ig * xc)
    a3 = a.reshape(ng, SUBLANES, LANES)
    u3 = u.reshape(ng, SUBLANES, LANES)
    row = lax.broadcasted_iota(jnp.int32, (ng, SUBLANES, LANES), 1)
    sh = 1
    while sh < SUBLANES:
        a_s = pltpu.roll(a3, sh, 1)
        u_s = pltpu.roll(u3, sh, 1)
        m = row >= sh
        u3 = jnp.where(m, a3 * u_s + u3, u3)
        a3 = jnp.where(m, a3 * a_s, a3)
        sh *= 2
    a_ref[...] = a3
    u_ref[...] = u3

    def body(i, h):
        hh = a_ref[i] * h + u_ref[i]
        u_ref[i] = hh
        return hh[SUBLANES - 1:SUBLANES, :]

    lax.fori_loop(0, ng, body, jnp.zeros((1, LANES), F32), unroll=8)
    o_ref[0] = (_gelu_tanh(ga_ref[0]) * u_ref[...].reshape(s, LANES)).astype(o_ref.dtype)


def _rglru(proj3, cw, cb, wr_bd, br, wi_bd, bi, lam):
    bsz, s, _ = proj3.shape
    c = cw.shape[1]
    nb = c // LANES
    vec = lambda b, j: (0, j)
    return pl.pallas_call(
        _rglru_kernel,
        grid=(bsz, nb),
        in_specs=[pl.BlockSpec((1, s, LANES), lambda b, j: (b, 0, j)),
                  pl.BlockSpec((1, s, LANES), lambda b, j: (b, 0, nb + j)),
                  pl.BlockSpec((CONV_WIDTH, LANES), vec),
                  pl.BlockSpec((1, LANES), vec),
                  pl.BlockSpec((LANES, LANES), lambda b, j: (j, j)),
                  pl.BlockSpec((1, LANES), vec),
                  pl.BlockSpec((LANES, LANES), lambda b, j: (j, j)),
                  pl.BlockSpec((1, LANES), vec),
                  pl.BlockSpec((1, LANES), vec)],
        out_specs=pl.BlockSpec((1, s, LANES), lambda b, j: (b, 0, j)),
        out_shape=jax.ShapeDtypeStruct((bsz, s, c), BF16),
        scratch_shapes=[pltpu.VMEM((s + SUBLANES, LANES), F32),
                        pltpu.VMEM((s // SUBLANES, SUBLANES, LANES), F32),
                        pltpu.VMEM((s // SUBLANES, SUBLANES, LANES), F32)],
        compiler_params=_params(2),
        name="rglru",
    )(proj3, proj3, cw, cb.reshape(1, c), wr_bd, br.reshape(1, c), wi_bd, bi.reshape(1, c),
      lam.reshape(1, c))


def _mlstm_kernel(bias_ref, q_ref, k_ref, v_ref, og_ref, gi_ref, gf_ref, cwq_ref, cbq_ref,
                  cwk_ref, cbk_ref, ng_ref, o_ref, pad_ref, q_sc, k_sc, li_sc, lf_sc, bc_sc):
    hd = pl.program_id(1)
    s = q_ref.shape[1]
    dh = q_ref.shape[2]
    nc = s // ML_CHUNK

    def put_q(r0, y):
        q_sc[r0:r0 + y.shape[0], :] = jax.nn.silu(y) * (dh ** -0.5)

    def put_k(r0, y):
        k_sc[r0:r0 + y.shape[0], :] = jax.nn.silu(y)

    _causal_conv_blocks(q_ref[0], cwq_ref[...], cbq_ref[...], pad_ref, put_q)
    _causal_conv_blocks(k_ref[0], cwk_ref[...], cbk_ref[...], pad_ref, put_k)
    li_sc[...] = gi_ref[0, 0] + bias_ref[hd]
    lf = -_softplus(-(gf_ref[0, 0] + bias_ref[ML_HEADS + hd]))
    lf_sc[...] = lf
    jj = lax.broadcasted_iota(jnp.int32, (ML_CHUNK, ML_CHUNK), 0)
    kk = lax.broadcasted_iota(jnp.int32, (ML_CHUNK, ML_CHUNK), 1)
    tri = kk <= jj
    eye = kk == jj
    upper = jnp.where(jj <= kk, 1.0, 0.0).astype(F32)
    bc_sc[...] = jnp.dot(lf, upper, precision=HIGHEST, preferred_element_type=F32)
    norm_g = ng_ref[...]

    def chunk(c, carry):
        c_st, n_st, m_st = carry
        r0 = pl.multiple_of(c * ML_CHUNK, ML_CHUNK)
        qc = q_sc[pl.ds(r0, ML_CHUNK), :]
        kc = k_sc[pl.ds(r0, ML_CHUNK), :]
        vc = v_ref[0, pl.ds(r0, ML_CHUNK), :]
        li_row = li_sc[pl.ds(c, 1), :]
        lf_row = lf_sc[pl.ds(c, 1), :]
        b_row = bc_sc[pl.ds(c, 1), :]
        b_col = jnp.sum(jnp.where(tri, lf_row, 0.0), axis=1, keepdims=True)
        li_col = jnp.sum(jnp.where(eye, li_row, 0.0), axis=1, keepdims=True)
        g_tot = jnp.sum(lf_row, axis=1, keepdims=True)
        w_col = g_tot - b_col + li_col
        m_loc = jnp.max(w_col, axis=0, keepdims=True)
        kw = kc * jnp.exp(w_col - m_loc)
        c_loc = jnp.dot(vc.T.astype(BF16), kw.astype(BF16), preferred_element_type=F32)
        n_loc = jnp.sum(kw, axis=0, keepdims=True)
        d = jnp.where(tri, b_col - b_row + li_row, NEG)
        m_inter = b_col + m_st
        m = jnp.maximum(m_inter, jnp.max(d, axis=1, keepdims=True))
        qb = qc.astype(BF16)
        qk = lax.dot_general(qb, kc.astype(BF16), NT_DIMS, preferred_element_type=F32)
        p = jnp.exp(d - m) * qk
        sc = jnp.exp(m_inter - m)
        inter = lax.dot_general(qb, c_st.astype(BF16), NT_DIMS, preferred_element_type=F32)
        num = sc * inter + jnp.dot(p.astype(BF16), vc.astype(BF16), preferred_element_type=F32)
        den = sc * jnp.sum(qc * n_st, axis=1, keepdims=True) + jnp.sum(p, axis=1, keepdims=True)
        h = num / jnp.maximum(jnp.abs(den), jnp.exp(-m))
        h = h * lax.rsqrt(jnp.mean(h * h, axis=1, keepdims=True) + EPS) * norm_g
        o_ref[0, pl.ds(r0, ML_CHUNK), :] = (
            jax.nn.sigmoid(og_ref[0, pl.ds(r0, ML_CHUNK), :]) * h).astype(o_ref.dtype)
        m_new = jnp.maximum(g_tot + m_st, m_loc)
        sa = jnp.exp(g_tot + m_st - m_new)
        sb = jnp.exp(m_loc - m_new)
        return sa * c_st + sb * c_loc, sa * n_st + sb * n_loc, m_new

    init = (jnp.zeros((dh, dh), F32), jnp.zeros((1, dh), F32), jnp.full((1, 1), NEG, F32))
    lax.fori_loop(0, nc, chunk, init, unroll=2)


def _mlstm(proj3, gates_t, bias, cw, cb, norm_g, *, col0):
    bsz, s, _ = proj3.shape
    nh = ML_HEADS
    dh = LANES
    nc = s // ML_CHUNK
    base = col0 // dh
    blk = lambda off: pl.BlockSpec((1, s, dh), lambda b, h: (b, 0, base + off + h))
    vec = lambda off: (lambda b, h: (0, off + h))
    return pl.pallas_call(
        _mlstm_kernel,
        grid=(bsz, nh),
        in_specs=[pl.BlockSpec(memory_space=pltpu.SMEM),
                  blk(0), blk(nh), blk(2 * nh), blk(3 * nh),
                  pl.BlockSpec((1, 1, nc, ML_CHUNK), lambda b, h: (b, h, 0, 0)),
                  pl.BlockSpec((1, 1, nc, ML_CHUNK), lambda b, h: (b, nh + h, 0, 0)),
                  pl.BlockSpec((CONV_WIDTH, dh), vec(0)), pl.BlockSpec((1, dh), vec(0)),
                  pl.BlockSpec((CONV_WIDTH, dh), vec(nh)), pl.BlockSpec((1, dh), vec(nh)),
                  pl.BlockSpec((1, dh), vec(0))],
        out_specs=pl.BlockSpec((1, s, dh), lambda b, h: (b, 0, h)),
        out_shape=jax.ShapeDtypeStruct((bsz, s, nh * dh), BF16),
        scratch_shapes=[pltpu.VMEM((s + SUBLANES, dh), F32),
                        pltpu.VMEM((s, dh), F32), pltpu.VMEM((s, dh), F32),
                        pltpu.VMEM((nc, ML_CHUNK), F32), pltpu.VMEM((nc, ML_CHUNK), F32),
                        pltpu.VMEM((nc, ML_CHUNK), F32)],
        compiler_params=_params(2),
        name="mlstm",
    )(bias, proj3, proj3, proj3, proj3, gates_t, gates_t, cw, cb.reshape(1, -1), cw,
      cb.reshape(1, -1), norm_g.reshape(1, -1))


def _ab_mixer(x, g, w_in, rg_conv_w, rg_conv_b, rg_w_r, rg_b_r, rg_w_i, rg_b_i, rg_lambda,
              ml_conv_w, ml_conv_b, ml_b_i, ml_b_f, ml_norm, w_out, bsz, s):
    d_rg = rg_conv_w.shape[1]
    d_ml = ml_norm.shape[0]
    d_main = 2 * d_rg + 4 * d_ml
    n_gate = 2 * ML_HEADS
    w_in_p = jnp.pad(w_in, ((0, 0), (0, LANES - n_gate))).astype(BF16)
    proj = _norm_matmul(x, g, w_in_p)
    proj3 = proj.reshape(bsz, s, proj.shape[1])
    bd = lambda w: jax.scipy.linalg.block_diag(*[w[i] for i in range(RG_BLOCKS)]).astype(BF16)
    ya = _rglru(proj3, rg_conv_w, rg_conv_b, bd(rg_w_r), rg_b_r, bd(rg_w_i), rg_b_i, rg_lambda)
    gates_t = proj3[:, :, d_main:d_main + n_gate].transpose(0, 2, 1).reshape(
        bsz, n_gate, s // ML_CHUNK, ML_CHUNK)
    yb = _mlstm(proj3, gates_t, jnp.concatenate([ml_b_i, ml_b_f]), ml_conv_w, ml_conv_b, ml_norm,
                col0=2 * d_rg)
    w_out_b = w_out.astype(BF16)
    return [(ya.reshape(bsz * s, d_rg), w_out_b[:d_rg]), (yb.reshape(bsz * s, d_ml), w_out_b[d_rg:])]


def _nsa_proj_kernel(x_ref, g_ref, w_ref, q_ref, gt_ref, kc_ref, vc_ref, ks_ref, kw_ref, vs_ref, vw_ref,
                     *, seq):
    tm = x_ref.shape[0]
    kv = NSA_GROUPS * NSA_HEAD_DIM
    width = NSA_HEADS * NSA_HEAD_DIM
    half = LANES // 2
    xn = _rms(x_ref[...], g_ref[...]).astype(BF16)
    proj = jnp.dot(xn, w_ref[...], preferred_element_type=F32)
    lane = lax.broadcasted_iota(jnp.int32, (tm, LANES), 1)
    lo = lane < half

    def lane_pair(c0):
        p = proj[:, c0:c0 + LANES]
        return p, pltpu.roll(p, half, 1)

    scale = NSA_HEAD_DIM ** -0.5 * LOG2E
    for pair in range(NSA_HEADS // 2):
        for j, piece in enumerate(lane_pair(pair * LANES)):
            h = 2 * pair + j
            q_ref[:, h * LANES:(h + 1) * LANES] = jnp.where(lo, piece * scale, 0.0).astype(BF16)
    gt_ref[0] = proj[:, width + 6 * kv:width + 6 * kv + LANES].T
    kc_ref[...] = proj[:, width:width + kv]
    vc_ref[...] = proj[:, width + kv:width + 2 * kv]
    pos = (pl.program_id(0) % (seq // tm)) * tm + lax.broadcasted_iota(jnp.int32, (tm, LANES), 0)
    onehot = jnp.where(lane == half + pos // SEL_BLOCK, 1.0, 0.0)
    for pair in range(NSA_GROUPS // 2):
        for idx, ref, fill in ((2, ks_ref, onehot), (4, kw_ref, 0.0)):
            for j, piece in enumerate(lane_pair(width + idx * kv + pair * LANES)):
                ref[0, 2 * pair + j] = jnp.where(lo, piece, fill).astype(BF16)
        for idx, ref in ((3, vs_ref), (5, vw_ref)):
            c0 = width + idx * kv + pair * LANES
            t = proj[:, c0:c0 + LANES].T.astype(BF16)
            ones = jnp.ones((NSA_VROWS - half, NSA_QTILE), BF16)
            for j in range(tm // NSA_QTILE):
                for hf in range(2):
                    ref[0, 2 * pair + hf, j, 0:half, :] = t[hf * half:(hf + 1) * half,
                                                            j * NSA_QTILE:(j + 1) * NSA_QTILE]
                    ref[0, 2 * pair + hf, j, half:NSA_VROWS, :] = ones


def _nsa_proj(x, g, w, bsz, s):
    n, d = x.shape
    dout = w.shape[1]
    tm = min(TOKEN_TILE, s)
    nsb = s // tm
    ng, dk = NSA_GROUPS, NSA_HEAD_DIM
    kv = ng * dk
    tq = NSA_QTILE
    const = lambda i: (0, 0)
    row = lambda i: (i, 0)
    keyed = lambda i: (i // nsb, 0, i % nsb, 0)
    return pl.pallas_call(
        functools.partial(_nsa_proj_kernel, seq=s),
        grid=(n // tm,),
        in_specs=[pl.BlockSpec((tm, d), row), pl.BlockSpec((1, d), const), _resident((d, dout), const)],
        out_specs=[pl.BlockSpec((tm, NSA_HEADS * LANES), row),
                   pl.BlockSpec((1, LANES, tm), lambda i: (i // nsb, 0, i % nsb)),
                   pl.BlockSpec((tm, kv), row), pl.BlockSpec((tm, kv), row),
                   pl.BlockSpec((1, ng, tm, LANES), keyed), pl.BlockSpec((1, ng, tm, LANES), keyed),
                   pl.BlockSpec((1, ng, tm // tq, NSA_VROWS, tq), lambda i: (i // nsb, 0, i % nsb, 0, 0)),
                   pl.BlockSpec((1, ng, tm // tq, NSA_VROWS, tq), lambda i: (i // nsb, 0, i % nsb, 0, 0))],
        out_shape=[jax.ShapeDtypeStruct((n, NSA_HEADS * LANES), BF16),
                   jax.ShapeDtypeStruct((bsz, LANES, s), F32),
                   jax.ShapeDtypeStruct((n, kv), F32), jax.ShapeDtypeStruct((n, kv), F32),
                   jax.ShapeDtypeStruct((bsz, ng, s, LANES), BF16),
                   jax.ShapeDtypeStruct((bsz, ng, s, LANES), BF16),
                   jax.ShapeDtypeStruct((bsz, ng, s // tq, NSA_VROWS, tq), BF16),
                   jax.ShapeDtypeStruct((bsz, ng, s // tq, NSA_VROWS, tq), BF16)],
        compiler_params=_params(1),
        name="nsa_proj",
    )(x, g.reshape(1, d), w)


def _compress_kernel(x_ref, pe_ref, w1_ref, b1_ref, w2_ref, o_ref, *, transpose_out):
    nb = x_ref.shape[1] // CMP_STRIDE
    top = bot = None
    for l in range(CMP_STRIDE):
        rows = x_ref[0, pl.ds(l, nb, stride=CMP_STRIDE), :]
        t = jnp.dot((rows + pe_ref[l:l + 1, :]).astype(BF16), w1_ref[l], preferred_element_type=F32)
        b = jnp.dot((rows + pe_ref[CMP_STRIDE + l:CMP_STRIDE + l + 1, :]).astype(BF16),
                    w1_ref[CMP_STRIDE + l], preferred_element_type=F32)
        top = t if top is None else top + t
        bot = b if bot is None else bot + b
    pre = top + pltpu.roll(bot, nb - 1, 0) + b1_ref[...]
    out = jnp.dot(_gelu_tanh(pre).astype(BF16), w2_ref[...], preferred_element_type=F32)
    out = jnp.where(lax.broadcasted_iota(jnp.int32, out.shape, 0) < nb - 1, out, 0.0)
    for hf in range(2):
        piece = out[:, hf * LANES:(hf + 1) * LANES]
        if transpose_out:
            o_ref[0, hf] = piece.T[0:NSA_HEAD_DIM, :].astype(o_ref.dtype)
        else:
            o_ref[0, hf] = piece.astype(o_ref.dtype)


def _compress(kv_in, pe, w1, b1, w2, *, transpose_out):
    bsz, s, _ = kv_in.shape
    dk, hid = NSA_HEAD_DIM, w1.shape[1]
    nb = s // CMP_STRIDE
    w1r = w1.reshape(CMP_BLOCK, dk, hid)
    z = jnp.zeros_like(w1r)
    w1p = jnp.concatenate([jnp.concatenate([w1r, z], axis=2), jnp.concatenate([z, w1r], axis=2)],
                          axis=1).astype(BF16)
    w2w = jnp.pad(w2, ((0, 0), (0, LANES - dk)))
    z2 = jnp.zeros_like(w2w)
    w2p = jnp.concatenate([jnp.concatenate([w2w, z2], axis=1), jnp.concatenate([z2, w2w], axis=1)],
                          axis=0).astype(BF16)
    const2 = lambda b, p: (0, 0)
    out_block = (1, 2, dk, nb) if transpose_out else (1, 2, nb, LANES)
    out_shape = (bsz, NSA_GROUPS, dk, nb) if transpose_out else (bsz, NSA_GROUPS, nb, LANES)
    return pl.pallas_call(
        functools.partial(_compress_kernel, transpose_out=transpose_out),
        grid=(bsz, NSA_GROUPS // 2),
        in_specs=[pl.BlockSpec((1, s, LANES), lambda b, p: (b, 0, p)),
                  pl.BlockSpec((CMP_BLOCK, LANES), const2),
                  _resident((CMP_BLOCK, LANES, 2 * hid), lambda b, p: (0, 0, 0)),
                  pl.BlockSpec((1, 2 * hid), const2),
                  _resident((2 * hid, 2 * LANES), const2)],
        out_specs=pl.BlockSpec(out_block, lambda b, p: (b, p, 0, 0)),
        out_shape=jax.ShapeDtypeStruct(out_shape, BF16),
        compiler_params=_params(2),
        name="nsa_compress",
    )(kv_in, jnp.concatenate([pe, pe], axis=1), w1p, jnp.concatenate([b1, b1]).reshape(1, 2 * hid), w2p)


def _nsa_kernel(bg_ref, q_ref, gt_ref, kc_ref, vc_ref, ks_ref, vs_ref, kw_ref, vw_ref, msel_ref,
                o_ref, qa_sc, s_sc, bias_sc, m_sc, acc_sc, part_sc):
    gp = pl.program_id(1)
    qi = pl.program_id(2)
    tq = q_ref.shape[0]
    rep = NSA_HEADS // NSA_GROUPS
    dk = NSA_HEAD_DIM
    half = LANES // 2
    n_sel = 32
    units = range(NSA_STEP_GROUPS)
    heads = range(rep)
    s0 = qi * tq
    key_off = lax.broadcasted_iota(jnp.int32, (tq, tq), 0)
    qry_off = lax.broadcasted_iota(jnp.int32, (tq, tq), 1)
    cols = lambda r: slice(r * tq, (r + 1) * tq)
    nt_dot = lambda k, q: lax.dot_general(k, q, NT_DIMS, preferred_element_type=F32)
    q_head = lambda u, r: q_ref[:, (u * rep + r) * LANES:(u * rep + r + 1) * LANES]

    def gate_row(u, r, br):
        idx = (gp * NSA_STEP_GROUPS + u) * (rep * NSA_BRANCHES) + r * NSA_BRANCHES + br
        return jax.nn.sigmoid(gt_ref[0, pl.ds(idx, 1), :] + bg_ref[idx])

    kp = jnp.maximum(qi - 1, 0)
    p0 = pl.multiple_of(kp * tq, tq)
    d0 = pl.multiple_of(qi * tq, tq)
    cmp_scores = [[nt_dot(kc_ref[0, u], q_head(u, r)) for r in heads] for u in units]
    win_scores = [[(nt_dot(kw_ref[0, u, pl.ds(p0, tq), :], q_head(u, r)),
                    nt_dot(kw_ref[0, u, pl.ds(d0, tq), :], q_head(u, r))) for r in heads] for u in units]
    for u in units:
        for r in heads:
            s_sc[1, u, r] = nt_dot(ks_ref[0, u, pl.ds(d0, tq), :], q_head(u, r))

    n_blk = kc_ref.shape[2]
    blk_id = lax.broadcasted_iota(jnp.int32, (n_blk, tq), 0)
    t_blk = s0 + lax.broadcasted_iota(jnp.int32, (n_blk, tq), 1)
    cmask = (blk_id * CMP_STRIDE + (CMP_BLOCK - 1) <= t_blk) & (blk_id < n_blk - 1)
    psum = [None] * NSA_STEP_GROUPS
    for r in heads:
        for u in units:
            scm = jnp.where(cmask, cmp_scores[u][r], NEG)
            e = jnp.where(cmask, jnp.exp2(scm - jnp.max(scm, axis=0, keepdims=True)), 0.0)
            den = jnp.sum(e, axis=0, keepdims=True)
            p = e * (1.0 / jnp.where(den > 0.0, den, 1.0))
            psum[u] = p if psum[u] is None else psum[u] + p
            part_sc[u, :, cols(r)] = gate_row(u, r, 0) * jnp.dot(vc_ref[0, u], p.astype(BF16),
                                                                 preferred_element_type=F32)

    jb = lax.broadcasted_iota(jnp.int32, (n_sel, tq), 0)
    jbf = jb.astype(F32)
    cur = (s0 + lax.broadcasted_iota(jnp.int32, (n_sel, tq), 1)) // SEL_BLOCK
    valid = jb <= cur
    forced = ((jb == 0) | (jb == cur) | (jb == cur - 1)) & valid
    for u in units:
        p_hi = psum[u].astype(BF16)
        p_lo = (psum[u] - p_hi.astype(F32)).astype(BF16)
        score_t = (jnp.dot(msel_ref[...], p_hi, preferred_element_type=F32)
                   + jnp.dot(msel_ref[...], p_lo, preferred_element_type=F32))
        s2 = jnp.where(forced, FORCED_SCORE, jnp.where(valid, score_t[half:half + n_sel, :], -1.0))
        for _ in range(SEL_TOPN):
            best = jnp.max(s2, axis=0, keepdims=True)
            first = jnp.min(jnp.where(s2 == best, jbf, float(n_sel)), axis=0, keepdims=True)
            s2 = jnp.where(jbf == first, TAKEN, s2)
        bias_t = jnp.where((s2 == TAKEN) & valid, 0.0, NEG)
        bias_sc[u] = bias_t
        bias_full = jnp.concatenate(
            [jnp.zeros((half, tq), F32), bias_t, jnp.zeros((LANES - half - n_sel, tq), F32)], axis=0)
        bias_q = bias_full.T
        for r in heads:
            qa_sc[u, r] = (q_head(u, r).astype(F32) + bias_q).astype(BF16)

    prev_mask = (key_off > qry_off) & (qi >= 1)
    diag_mask = key_off <= qry_off
    for r in heads:
        for u in units:
            s_prev = jnp.where(prev_mask, win_scores[u][r][0], NEG)
            s_diag = jnp.where(diag_mask, win_scores[u][r][1], NEG)
            mw = jnp.maximum(jnp.max(s_prev, axis=0, keepdims=True), jnp.max(s_diag, axis=0, keepdims=True))
            win = (jnp.dot(vw_ref[0, u, kp], jnp.exp2(s_prev - mw).astype(BF16), preferred_element_type=F32)
                   + jnp.dot(vw_ref[0, u, qi], jnp.exp2(s_diag - mw).astype(BF16),
                             preferred_element_type=F32))
            part_sc[u, :, cols(r)] += gate_row(u, r, 2) * (win[0:dk] * (1.0 / win[dk:dk + 1]))

    blk0 = qi * (tq // SEL_BLOCK)
    for u in units:
        bias_diag = jnp.concatenate(
            [jnp.broadcast_to(bias_sc[u, pl.ds(blk0 + jj, 1), :], (SEL_BLOCK, tq))
             for jj in range(tq // SEL_BLOCK)], axis=0)
        for r in heads:
            s = jnp.where(diag_mask, s_sc[1, u, r] + bias_diag, NEG)
            m0 = jnp.max(s, axis=0, keepdims=True)
            m_sc[u, :, cols(r)] = m0
            acc_sc[u, :, cols(r)] = jnp.dot(vs_ref[0, u, qi], jnp.exp2(s - m0).astype(BF16),
                                            preferred_element_type=F32)

    def qk_into(kt, slot):
        k0 = pl.multiple_of(kt * tq, tq)
        for u in units:
            k_tile = ks_ref[0, u, pl.ds(k0, tq), :]
            for r in heads:
                s_sc[slot, u, r] = nt_dot(k_tile, qa_sc[u, r])

    def softmax_pv(kt, slot):
        for u in units:
            v_t = vs_ref[0, u, kt]
            for r in heads:
                s = s_sc[slot, u, r]
                m_old = m_sc[u, :, cols(r)]
                m_new = jnp.maximum(m_old, jnp.max(s, axis=0, keepdims=True))
                alpha = jnp.exp2(m_old - m_new)
                p = jnp.exp2(s - m_new)
                acc_sc[u, :, cols(r)] = alpha * acc_sc[u, :, cols(r)] + jnp.dot(
                    v_t, p.astype(BF16), preferred_element_type=F32)
                m_sc[u, :, cols(r)] = m_new

    qk_into(0, 0)

    def pair_body(j, carry):
        a = 2 * j
        qk_into(a + 1, 1)
        softmax_pv(a, 0)
        qk_into(a + 2, 0)
        softmax_pv(a + 1, 1)
        return carry

    lax.fori_loop(0, qi // 2, pair_body, 0)

    @pl.when(qi % 2 == 1)
    def _():
        softmax_pv(qi - 1, 0)

    for u in units:
        merged = []
        for r in heads:
            o_slc = acc_sc[u, 0:dk, cols(r)] * (1.0 / acc_sc[u, dk:dk + 1, cols(r)])
            merged.append(part_sc[u, :, cols(r)] + gate_row(u, r, 1) * o_slc)
        for pair in range(rep // 2):
            both = jnp.concatenate([merged[2 * pair], merged[2 * pair + 1]], axis=0)
            c0 = (u * (rep // 2) + pair) * LANES
            o_ref[:, c0:c0 + LANES] = both.T.astype(o_ref.dtype)


def _nsa_attention(q, gt, b_gate, kc, vc_t, ks, vs_t, kw, vw_t, msel_t, bsz, s):
    n = q.shape[0]
    tq = NSA_QTILE
    assert tq == WINDOW and s % tq == 0 and s // SEL_BLOCK == 32
    rep = NSA_HEADS // NSA_GROUPS
    dk = NSA_HEAD_DIM
    nq = s // tq
    ug = NSA_STEP_GROUPS
    per_bg = lambda a: pl.BlockSpec((1, ug) + a.shape[2:], lambda b, g, i: (b, g) + (0,) * (a.ndim - 2))
    return pl.pallas_call(
        _nsa_kernel,
        grid=(bsz, NSA_GROUPS // ug, nq),
        in_specs=[pl.BlockSpec(memory_space=pltpu.SMEM),
                  pl.BlockSpec((tq, ug * rep * LANES), lambda b, g, i: (b * nq + i, g)),
                  pl.BlockSpec((1, LANES, tq), lambda b, g, i: (b, 0, i)),
                  per_bg(kc), per_bg(vc_t), per_bg(ks), per_bg(vs_t), per_bg(kw), per_bg(vw_t),
                  pl.BlockSpec(msel_t.shape, lambda b, g, i: (0, 0))],
        out_specs=pl.BlockSpec((tq, ug * rep * dk), lambda b, g, i: (b * nq + i, g)),
        out_shape=jax.ShapeDtypeStruct((n, NSA_HEADS * dk), BF16),
        scratch_shapes=[pltpu.VMEM((ug, rep, tq, LANES), BF16),
                        pltpu.VMEM((2, ug, rep, tq, tq), F32),
                        pltpu.VMEM((ug, s // SEL_BLOCK, tq), F32),
                        pltpu.VMEM((ug, 1, rep * tq), F32),
                        pltpu.VMEM((ug, NSA_VROWS, rep * tq), F32),
                        pltpu.VMEM((ug, dk, rep * tq), F32)],
        compiler_params=_params(3),
        name="nsa_attention",
    )(b_gate, q, gt, kc, vc_t, ks, vs_t, kw, vw_t, msel_t)


def _sel_matrix_t(n_rows, n_sel):
    n_cmp = np.arange(n_rows)[None, :] * CMP_STRIDE
    sel = np.arange(n_sel)[:, None] * SEL_BLOCK
    ov = np.clip(np.minimum(n_cmp + CMP_BLOCK, sel + SEL_BLOCK) - np.maximum(n_cmp, sel), 0, None)
    out = np.zeros((LANES, n_rows), np.float32)
    out[LANES // 2:LANES // 2 + n_sel] = ov / CMP_BLOCK
    return jnp.asarray(out, dtype=BF16)


def _nsa_mixer(x, g, w_in, pe_k, k_w1, k_b1, k_w2, pe_v, v_w1, v_b1, v_w2, b_gate, w_out, bsz, s):
    n_gate = NSA_BRANCHES * NSA_HEADS
    kv = NSA_GROUPS * NSA_HEAD_DIM
    w_in_p = jnp.pad(w_in, ((0, 0), (0, LANES - n_gate))).astype(BF16)
    q, gt, kc_in, vc_in, ks, kw, vs_t, vw_t = _nsa_proj(x, g, w_in_p, bsz, s)
    kc = _compress(kc_in.reshape(bsz, s, kv), pe_k, k_w1, k_b1, k_w2, transpose_out=False)
    vc_t = _compress(vc_in.reshape(bsz, s, kv), pe_v, v_w1, v_b1, v_w2, transpose_out=True)
    msel_t = _sel_matrix_t(s // CMP_STRIDE, s // SEL_BLOCK)
    attn = _nsa_attention(q, gt, b_gate, kc, vc_t, ks, vs_t, kw, vw_t, msel_t, bsz, s)
    return [(attn, w_out.astype(BF16))]


def kernel(x, ffn1_norm, ffn1_w_gu, ffn1_w_down, mix_norm, ffn2_norm, ffn2_w_gu, ffn2_w_down,
           ab_w_in, rg_conv_w, rg_conv_b, rg_w_r, rg_b_r, rg_w_i, rg_b_i, rg_lambda,
           ml_conv_w, ml_conv_b, ml_b_i, ml_b_f, ml_norm, ab_w_out,
           nsa_w_in, nsa_pe_k, nsa_k_w1, nsa_k_b1, nsa_k_w2, nsa_pe_v, nsa_v_w1, nsa_v_b1, nsa_v_w2,
           nsa_b_gate, nsa_w_out, final_norm):
    bsz, s, d = x.shape
    depth = ffn1_norm.shape[0]
    h = x.reshape(bsz * s, d)
    for i in range(depth):
        j = i // 2
        h = _ffn(h, [], ffn1_norm[i], ffn1_w_gu[i].astype(BF16), ffn1_w_down[i].astype(BF16),
                 final_norm, final=False)
        if i % 2 == 0:
            mix = _ab_mixer(h, mix_norm[i], ab_w_in[j], rg_conv_w[j], rg_conv_b[j], rg_w_r[j], rg_b_r[j],
                            rg_w_i[j], rg_b_i[j], rg_lambda[j], ml_conv_w[j], ml_conv_b[j], ml_b_i[j],
                            ml_b_f[j], ml_norm[j], ab_w_out[j], bsz, s)
        else:
            mix = _nsa_mixer(h, mix_norm[i], nsa_w_in[j], nsa_pe_k[j], nsa_k_w1[j], nsa_k_b1[j],
                             nsa_k_w2[j], nsa_pe_v[j], nsa_v_w1[j], nsa_v_b1[j], nsa_v_w2[j],
                             nsa_b_gate[j], nsa_w_out[j], bsz, s)
        h = _ffn(h, mix, ffn2_norm[i], ffn2_w_gu[i].astype(BF16), ffn2_w_down[i].astype(BF16),
                 final_norm, final=(i == depth - 1))
    return h.reshape(bsz, s, d)
```

```python
import functools

import jax
import jax.numpy as jnp
import numpy as np
from jax import lax
from jax.experimental import pallas as pl
from jax.experimental.pallas import tpu as pltpu

F32 = jnp.float32
BF16 = jnp.bfloat16
HIGHEST = lax.Precision.HIGHEST

EPS = 1e-6
RG_C = 8.0
RG_BLOCKS = 8
CONV_WIDTH = 4
ML_HEADS = 4
ML_CHUNK = 128
NSA_HEADS = 16
NSA_GROUPS = 4
NSA_HEAD_DIM = 64
NSA_BRANCHES = 3
CMP_BLOCK = 32
CMP_STRIDE = 16
SEL_BLOCK = 64
SEL_TOPN = 8
WINDOW = 256
FORCED_SCORE = 1e6

LANES = 128
SUBLANES = 8
MXU_TILE = 256
NEG = -1e30
TAKEN = -2.0
VMEM_LIMIT = 48 * 1024 * 1024

TOKEN_TILE = 512
NSA_QTILE = 256
NSA_VROWS = 80
NSA_STEP_GROUPS = 2
LOG2E = 1.4426950408889634
NT_DIMS = (((1,), (1,)), ((), ()))


def _params(n_axes):
    return pltpu.CompilerParams(dimension_semantics=("arbitrary",) * n_axes,
                                vmem_limit_bytes=VMEM_LIMIT)


def _resident(shape, index_map):
    return pl.BlockSpec(shape, index_map, pipeline_mode=pl.Buffered(1))


def _rms(x, g):
    return x * lax.rsqrt(jnp.mean(x * x, axis=-1, keepdims=True) + EPS) * g


def _gelu_tanh(x):
    return 0.5 * x * (1.0 + jnp.tanh(0.7978845608028654 * (x + 0.044715 * (x * x * x))))


def _softplus(z):
    return jnp.maximum(z, 0.0) + jnp.log1p(jnp.exp(-jnp.abs(z)))


def _sigmoid(z):
    return 0.5 * jnp.tanh(0.5 * z) + 0.5


def _sqrt_nonneg(z):
    return jnp.where(z > 0.0, z * lax.rsqrt(z), 0.0)


def _ffn_kernel(*refs, n_mix, n_chunks, final):
    x_ref = refs[0]
    a_refs = refs[1:1 + n_mix]
    wo_refs = refs[1 + n_mix:1 + 2 * n_mix]
    g_ref, wg_ref, wu_ref, wd_ref, gf_ref, o_ref = refs[1 + 2 * n_mix:]
    x = x_ref[...]
    for a_ref, wo_ref in zip(a_refs, wo_refs):
        x = x + jnp.dot(a_ref[...], wo_ref[...], preferred_element_type=F32)
    xn = _rms(x, g_ref[...]).astype(BF16)
    n_tiles = wg_ref.shape[1] // MXU_TILE
    bounds = [MXU_TILE * ((n_tiles * c + n_chunks - 1) // n_chunks) for c in range(n_chunks + 1)]
    acc = None
    for lo, hi in zip(bounds[:-1], bounds[1:]):
        gate = jnp.dot(xn, wg_ref[:, lo:hi], preferred_element_type=F32)
        up = jnp.dot(xn, wu_ref[:, lo:hi], preferred_element_type=F32)
        h = (gate * jax.nn.sigmoid(gate) * up).astype(BF16)
        part = jnp.dot(h, wd_ref[lo:hi, :], preferred_element_type=F32)
        acc = part if acc is None else acc + part
    y = x + 0.5 * acc
    if final:
        y = _rms(y, gf_ref[...])
    o_ref[...] = y


def _ffn(x, mix, g, w_gu, w_down, gf, *, final):
    n, d = x.shape
    f = w_down.shape[0]
    tm = min(TOKEN_TILE, n)
    const = lambda i: (0, 0)
    row = lambda i: (i, 0)
    acts = [a for a, _ in mix]
    wos = [w for _, w in mix]
    return pl.pallas_call(
        functools.partial(_ffn_kernel, n_mix=len(mix), n_chunks=2, final=final),
        grid=(n // tm,),
        in_specs=[pl.BlockSpec((tm, d), row)]
        + [pl.BlockSpec((tm, a.shape[1]), row) for a in acts]
        + [_resident(w.shape, const) for w in wos]
        + [pl.BlockSpec((1, d), const),
           _resident((d, f), const),
           _resident((d, f), lambda i: (0, 1)),
           _resident((f, d), const),
           pl.BlockSpec((1, d), const)],
        out_specs=pl.BlockSpec((tm, d), row),
        out_shape=jax.ShapeDtypeStruct((n, d), F32),
        compiler_params=_params(1),
        name="ffn_final" if final else ("ffn_mix" if mix else "ffn"),
    )(x, *acts, *wos, g.reshape(1, d), w_gu, w_gu, w_down, gf.reshape(1, d))


def _norm_matmul_kernel(x_ref, g_ref, w_ref, o_ref):
    xn = _rms(x_ref[...], g_ref[...]).astype(BF16)
    o_ref[...] = jnp.dot(xn, w_ref[...], preferred_element_type=F32)


def _norm_matmul(x, g, w):
    n, d = x.shape
    dout = w.shape[1]
    tm = min(TOKEN_TILE, n)
    const = lambda i: (0, 0)
    return pl.pallas_call(
        _norm_matmul_kernel,
        grid=(n // tm,),
        in_specs=[pl.BlockSpec((tm, d), lambda i: (i, 0)),
                  pl.BlockSpec((1, d), const),
                  _resident((d, dout), const)],
        out_specs=pl.BlockSpec((tm, dout), lambda i: (i, 0)),
        out_shape=jax.ShapeDtypeStruct((n, dout), F32),
        compiler_params=_params(1),
        name="norm_matmul",
    )(x, g.reshape(1, d), w)


def _causal_conv(x, cw, cb, pad_ref):
    s = x.shape[0]
    pad_ref[0:SUBLANES, :] = jnp.zeros((SUBLANES, x.shape[1]), F32)
    pad_ref[SUBLANES:SUBLANES + s, :] = x
    y = cb + cw[CONV_WIDTH - 1:CONV_WIDTH, :] * x
    for j in range(CONV_WIDTH - 1):
        off = SUBLANES - (CONV_WIDTH - 1 - j)
        y = y + cw[j:j + 1, :] * pad_ref[off:off + s, :]
    return y


def _causal_conv_blocks(x, cw, cb, pad_ref, emit, block=256):
    s = x.shape[0]
    pad_ref[0:SUBLANES, :] = jnp.zeros((SUBLANES, x.shape[1]), F32)
    pad_ref[SUBLANES:SUBLANES + s, :] = x
    for r0 in range(0, s, block):
        y = cb + cw[CONV_WIDTH - 1:CONV_WIDTH, :] * pad_ref[SUBLANES + r0:SUBLANES + r0 + block, :]
        for j in range(CONV_WIDTH - 1):
            off = r0 + SUBLANES - (CONV_WIDTH - 1 - j)
            y = y + cw[j:j + 1, :] * pad_ref[off:off + block, :]
        emit(r0, y)


def _rglru_kernel(xa_ref, ga_ref, cw_ref, cb_ref, wr_ref, br_ref, wi_ref, bi_ref, lam_ref,
                  o_ref, pad_ref, a_ref, u_ref):
    s = xa_ref.shape[1]
    ng = s // SUBLANES
    xc = _causal_conv(xa_ref[0], cw_ref[...], cb_ref[...], pad_ref)
    xb = xc.astype(BF16)
    r = _sigmoid(jnp.dot(xb, wr_ref[...], preferred_element_type=F32) + br_ref[...])
    ig = _sigmoid(jnp.dot(xb, wi_ref[...], preferred_element_type=F32) + bi_ref[...])
    log_a = -RG_C * r * _softplus(-lam_ref[...])
    a = jnp.exp(log_a)
    th = jnp.tanh(log_a)
    u = _sqrt_nonneg(-2.0 * th / (1.0 - th)) * (ig * xc)
    a3 = a.reshape(ng, SUBLANES, LANES)
    u3 = u.reshape(ng, SUBLANES, LANES)
    row = lax.broadcasted_iota(jnp.int32, (ng, SUBLANES, LANES), 1)
    sh = 1
    while sh < SUBLANES:
        a_s = pltpu.roll(a3, sh, 1)
        u_s = pltpu.roll(u3, sh, 1)
        m = row >= sh
        u3 = jnp.where(m, a3 * u_s + u3, u3)
        a3 = jnp.where(m, a3 * a_s, a3)
        sh *= 2
    a_ref[...] = a3
    u_ref[...] = u3

    def body(i, h):
        hh = a_ref[i] * h + u_ref[i]
        u_ref[i] = hh
        return hh[SUBLANES - 1:SUBLANES, :]

    lax.fori_loop(0, ng, body, jnp.zeros((1, LANES), F32), unroll=8)
    o_ref[0] = (_gelu_tanh(ga_ref[0]) * u_ref[...].reshape(s, LANES)).astype(o_ref.dtype)


def _rglru(proj3, cw, cb, wr_bd, br, wi_bd, bi, lam):
    bsz, s, _ = proj3.shape
    c = cw.shape[1]
    nb = c // LANES
    vec = lambda b, j: (0, j)
    return pl.pallas_call(
        _rglru_kernel,
        grid=(bsz, nb),
        in_specs=[pl.BlockSpec((1, s, LANES), lambda b, j: (b, 0, j)),
                  pl.BlockSpec((1, s, LANES), lambda b, j: (b, 0, nb + j)),
                  pl.BlockSpec((CONV_WIDTH, LANES), vec),
                  pl.BlockSpec((1, LANES), vec),
                  pl.BlockSpec((LANES, LANES), lambda b, j: (j, j)),
                  pl.BlockSpec((1, LANES), vec),
                  pl.BlockSpec((LANES, LANES), lambda b, j: (j, j)),
                  pl.BlockSpec((1, LANES), vec),
                  pl.BlockSpec((1, LANES), vec)],
        out_specs=pl.BlockSpec((1, s, LANES), lambda b, j: (b, 0, j)),
        out_shape=jax.ShapeDtypeStruct((bsz, s, c), BF16),
        scratch_shapes=[pltpu.VMEM((s + SUBLANES, LANES), F32),
                        pltpu.VMEM((s // SUBLANES, SUBLANES, LANES), F32),
                        pltpu.VMEM((s // SUBLANES, SUBLANES, LANES), F32)],
        compiler_params=_params(2),
        name="rglru",
    )(proj3, proj3, cw, cb.reshape(1, c), wr_bd, br.reshape(1, c), wi_bd, bi.reshape(1, c),
      lam.reshape(1, c))


def _mlstm_kernel(bias_ref, q_ref, k_ref, v_ref, og_ref, gi_ref, gf_ref, cwq_ref, cbq_ref,
                  cwk_ref, cbk_ref, ng_ref, o_ref, pad_ref, q_sc, k_sc, li_sc, lf_sc, bc_sc):
    hd = pl.program_id(1)
    s = q_ref.shape[1]
    dh = q_ref.shape[2]
    nc = s // ML_CHUNK

    def put_q(r0, y):
        q_sc[r0:r0 + y.shape[0], :] = jax.nn.silu(y) * (dh ** -0.5)

    def put_k(r0, y):
        k_sc[r0:r0 + y.shape[0], :] = jax.nn.silu(y)

    _causal_conv_blocks(q_ref[0], cwq_ref[...], cbq_ref[...], pad_ref, put_q)
    _causal_conv_blocks(k_ref[0], cwk_ref[...], cbk_ref[...], pad_ref, put_k)
    li_sc[...] = gi_ref[0, 0] + bias_ref[hd]
    lf = -_softplus(-(gf_ref[0, 0] + bias_ref[ML_HEADS + hd]))
    lf_sc[...] = lf
    jj = lax.broadcasted_iota(jnp.int32, (ML_CHUNK, ML_CHUNK), 0)
    kk = lax.broadcasted_iota(jnp.int32, (ML_CHUNK, ML_CHUNK), 1)
    tri = kk <= jj
    upper = jnp.where(jj <= kk, 1.0, 0.0).astype(F32)
    bc_sc[...] = jnp.dot(lf, upper, precision=HIGHEST, preferred_element_type=F32)
    norm_g = ng_ref[...]

    def chunk(c, carry):
        c_st, n_st, m_st = carry
        r0 = pl.multiple_of(c * ML_CHUNK, ML_CHUNK)
        qc = q_sc[pl.ds(r0, ML_CHUNK), :]
        kc = k_sc[pl.ds(r0, ML_CHUNK), :]
        vc = v_ref[0, pl.ds(r0, ML_CHUNK), :]
        li_row = li_sc[pl.ds(c, 1), :]
        lf_row = lf_sc[pl.ds(c, 1), :]
        b_row = bc_sc[pl.ds(c, 1), :]
        b_col = jnp.sum(jnp.where(tri, lf_row, 0.0), axis=1, keepdims=True)
        g_tot = jnp.sum(lf_row, axis=1, keepdims=True)
        w_row = g_tot - b_row + li_row
        m_loc = jnp.max(w_row, axis=1, keepdims=True)
        wk = jnp.exp(w_row - m_loc)
        lhs = jnp.concatenate([vc.T * wk, jnp.broadcast_to(wk, (2 * SUBLANES, ML_CHUNK))], axis=0)
        cn = jnp.dot(lhs.astype(BF16), kc.astype(BF16), preferred_element_type=F32)
        c_loc = cn[0:dh]
        n_loc = cn[dh:dh + 1]
        d = jnp.where(tri, b_col - b_row + li_row, NEG)
        m_inter = b_col + m_st
        m = jnp.maximum(m_inter, jnp.max(d, axis=1, keepdims=True))
        qb = qc.astype(BF16)
        qk = lax.dot_general(qb, kc.astype(BF16), NT_DIMS, preferred_element_type=F32)
        p = jnp.exp(d - m) * qk
        sc = jnp.exp(m_inter - m)
        inter = lax.dot_general(qb, c_st.astype(BF16), NT_DIMS, preferred_element_type=F32)
        num = sc * inter + jnp.dot(p.astype(BF16), vc.astype(BF16), preferred_element_type=F32)
        den = jnp.sum(sc * (qc * n_st) + p, axis=1, keepdims=True)
        h = num / jnp.maximum(jnp.abs(den), jnp.exp(-m))
        h = h * lax.rsqrt(jnp.mean(h * h, axis=1, keepdims=True) + EPS) * norm_g
        o_ref[0, pl.ds(r0, ML_CHUNK), :] = (
            jax.nn.sigmoid(og_ref[0, pl.ds(r0, ML_CHUNK), :]) * h).astype(o_ref.dtype)
        m_new = jnp.maximum(g_tot + m_st, m_loc)
        sa = jnp.exp(g_tot + m_st - m_new)
        sb = jnp.exp(m_loc - m_new)
        return sa * c_st + sb * c_loc, sa * n_st + sb * n_loc, m_new

    init = (jnp.zeros((dh, dh), F32), jnp.zeros((1, dh), F32), jnp.full((1, 1), NEG, F32))
    lax.fori_loop(0, nc, chunk, init, unroll=4)


def _mlstm(proj3, gates_t, bias, cw, cb, norm_g, *, col0):
    bsz, s, _ = proj3.shape
    nh = ML_HEADS
    dh = LANES
    nc = s // ML_CHUNK
    base = col0 // dh
    blk = lambda off: pl.BlockSpec((1, s, dh), lambda b, h: (b, 0, base + off + h))
    vec = lambda off: (lambda b, h: (0, off + h))
    return pl.pallas_call(
        _mlstm_kernel,
        grid=(bsz, nh),
        in_specs=[pl.BlockSpec(memory_space=pltpu.SMEM),
                  blk(0), blk(nh), blk(2 * nh), blk(3 * nh),
                  pl.BlockSpec((1, 1, nc, ML_CHUNK), lambda b, h: (b, h, 0, 0)),
                  pl.BlockSpec((1, 1, nc, ML_CHUNK), lambda b, h: (b, nh + h, 0, 0)),
                  pl.BlockSpec((CONV_WIDTH, dh), vec(0)), pl.BlockSpec((1, dh), vec(0)),
                  pl.BlockSpec((CONV_WIDTH, dh), vec(nh)), pl.BlockSpec((1, dh), vec(nh)),
                  pl.BlockSpec((1, dh), vec(0))],
        out_specs=pl.BlockSpec((1, s, dh), lambda b, h: (b, 0, h)),
        out_shape=jax.ShapeDtypeStruct((bsz, s, nh * dh), BF16),
        scratch_shapes=[pltpu.VMEM((s + SUBLANES, dh), F32),
                        pltpu.VMEM((s, dh), F32), pltpu.VMEM((s, dh), F32),
                        pltpu.VMEM((nc, ML_CHUNK), F32), pltpu.VMEM((nc, ML_CHUNK), F32),
                        pltpu.VMEM((nc, ML_CHUNK), F32)],
        compiler_params=_params(2),
        name="mlstm",
    )(bias, proj3, proj3, proj3, proj3, gates_t, gates_t, cw, cb.reshape(1, -1), cw,
      cb.reshape(1, -1), norm_g.reshape(1, -1))


def _ab_mixer(x, g, w_in, rg_conv_w, rg_conv_b, rg_w_r, rg_b_r, rg_w_i, rg_b_i, rg_lambda,
              ml_conv_w, ml_conv_b, ml_b_i, ml_b_f, ml_norm, w_out, bsz, s):
    d_rg = rg_conv_w.shape[1]
    d_ml = ml_norm.shape[0]
    d_main = 2 * d_rg + 4 * d_ml
    n_gate = 2 * ML_HEADS
    w_in_p = jnp.pad(w_in, ((0, 0), (0, LANES - n_gate))).astype(BF16)
    proj = _norm_matmul(x, g, w_in_p)
    proj3 = proj.reshape(bsz, s, proj.shape[1])
    bd = lambda w: jax.scipy.linalg.block_diag(*[w[i] for i in range(RG_BLOCKS)]).astype(BF16)
    ya = _rglru(proj3, rg_conv_w, rg_conv_b, bd(rg_w_r), rg_b_r, bd(rg_w_i), rg_b_i, rg_lambda)
    gates_t = proj3[:, :, d_main:d_main + n_gate].transpose(0, 2, 1).reshape(
        bsz, n_gate, s // ML_CHUNK, ML_CHUNK)
    yb = _mlstm(proj3, gates_t, jnp.concatenate([ml_b_i, ml_b_f]), ml_conv_w, ml_conv_b, ml_norm,
                col0=2 * d_rg)
    w_out_b = w_out.astype(BF16)
    return [(ya.reshape(bsz * s, d_rg), w_out_b[:d_rg]), (yb.reshape(bsz * s, d_ml), w_out_b[d_rg:])]


def _nsa_proj_kernel(x_ref, g_ref, w_ref, q_ref, gt_ref, kc_ref, vc_ref, ks_ref, kw_ref, vs_ref, vw_ref,
                     *, seq):
    tm = x_ref.shape[0]
    kv = NSA_GROUPS * NSA_HEAD_DIM
    width = NSA_HEADS * NSA_HEAD_DIM
    half = LANES // 2
    xn = _rms(x_ref[...], g_ref[...]).astype(BF16)
    proj = jnp.dot(xn, w_ref[...], preferred_element_type=F32)
    lane = lax.broadcasted_iota(jnp.int32, (tm, LANES), 1)
    lo = lane < half

    def lane_pair(c0):
        p = proj[:, c0:c0 + LANES]
        return p, pltpu.roll(p, half, 1)

    scale = NSA_HEAD_DIM ** -0.5 * LOG2E
    for pair in range(NSA_HEADS // 2):
        for j, piece in enumerate(lane_pair(pair * LANES)):
            h = 2 * pair + j
            q_ref[:, h * LANES:(h + 1) * LANES] = jnp.where(lo, piece * scale, 0.0).astype(BF16)
    gt_ref[0] = proj[:, width + 6 * kv:width + 6 * kv + LANES].T
    kc_ref[...] = proj[:, width:width + kv]
    vc_ref[...] = proj[:, width + kv:width + 2 * kv]
    pos = (pl.program_id(0) % (seq // tm)) * tm + lax.broadcasted_iota(jnp.int32, (tm, LANES), 0)
    onehot = jnp.where(lane == half + pos // SEL_BLOCK, 1.0, 0.0)
    for pair in range(NSA_GROUPS // 2):
        for idx, ref, fill in ((2, ks_ref, onehot), (4, kw_ref, 0.0)):
            for j, piece in enumerate(lane_pair(width + idx * kv + pair * LANES)):
                ref[0, 2 * pair + j] = jnp.where(lo, piece, fill).astype(BF16)
        for idx, ref in ((3, vs_ref), (5, vw_ref)):
            c0 = width + idx * kv + pair * LANES
            t = proj[:, c0:c0 + LANES].T.astype(BF16)
            ones = jnp.ones((NSA_VROWS - half, NSA_QTILE), BF16)
            for j in range(tm // NSA_QTILE):
                for hf in range(2):
                    ref[0, 2 * pair + hf, j, 0:half, :] = t[hf * half:(hf + 1) * half,
                                                            j * NSA_QTILE:(j + 1) * NSA_QTILE]
                    ref[0, 2 * pair + hf, j, half:NSA_VROWS, :] = ones


def _nsa_proj(x, g, w, bsz, s):
    n, d = x.shape
    dout = w.shape[1]
    tm = min(TOKEN_TILE, s)
    nsb = s // tm
    ng, dk = NSA_GROUPS, NSA_HEAD_DIM
    kv = ng * dk
    tq = NSA_QTILE
    const = lambda i: (0, 0)
    row = lambda i: (i, 0)
    keyed = lambda i: (i // nsb, 0, i % nsb, 0)
    return pl.pallas_call(
        functools.partial(_nsa_proj_kernel, seq=s),
        grid=(n // tm,),
        in_specs=[pl.BlockSpec((tm, d), row), pl.BlockSpec((1, d), const), _resident((d, dout), const)],
        out_specs=[pl.BlockSpec((tm, NSA_HEADS * LANES), row),
                   pl.BlockSpec((1, LANES, tm), lambda i: (i // nsb, 0, i % nsb)),
                   pl.BlockSpec((tm, kv), row), pl.BlockSpec((tm, kv), row),
                   pl.BlockSpec((1, ng, tm, LANES), keyed), pl.BlockSpec((1, ng, tm, LANES), keyed),
                   pl.BlockSpec((1, ng, tm // tq, NSA_VROWS, tq), lambda i: (i // nsb, 0, i % nsb, 0, 0)),
                   pl.BlockSpec((1, ng, tm // tq, NSA_VROWS, tq), lambda i: (i // nsb, 0, i % nsb, 0, 0))],
        out_shape=[jax.ShapeDtypeStruct((n, NSA_HEADS * LANES), BF16),
                   jax.ShapeDtypeStruct((bsz, LANES, s), F32),
                   jax.ShapeDtypeStruct((n, kv), F32), jax.ShapeDtypeStruct((n, kv), F32),
                   jax.ShapeDtypeStruct((bsz, ng, s, LANES), BF16),
                   jax.ShapeDtypeStruct((bsz, ng, s, LANES), BF16),
                   jax.ShapeDtypeStruct((bsz, ng, s // tq, NSA_VROWS, tq), BF16),
                   jax.ShapeDtypeStruct((bsz, ng, s // tq, NSA_VROWS, tq), BF16)],
        compiler_params=_params(1),
        name="nsa_proj",
    )(x, g.reshape(1, d), w)


def _compress_kernel(x_ref, pe_ref, w1_ref, b1_ref, w2_ref, o_ref, *, transpose_out):
    nb = x_ref.shape[1] // CMP_STRIDE
    top = bot = None
    for l in range(CMP_STRIDE):
        rows = x_ref[0, pl.ds(l, nb, stride=CMP_STRIDE), :]
        t = jnp.dot((rows + pe_ref[l:l + 1, :]).astype(BF16), w1_ref[l], preferred_element_type=F32)
        b = jnp.dot((rows + pe_ref[CMP_STRIDE + l:CMP_STRIDE + l + 1, :]).astype(BF16),
                    w1_ref[CMP_STRIDE + l], preferred_element_type=F32)
        top = t if top is None else top + t
        bot = b if bot is None else bot + b
    pre = top + pltpu.roll(bot, nb - 1, 0) + b1_ref[...]
    out = jnp.dot(_gelu_tanh(pre).astype(BF16), w2_ref[...], preferred_element_type=F32)
    out = jnp.where(lax.broadcasted_iota(jnp.int32, out.shape, 0) < nb - 1, out, 0.0)
    for hf in range(2):
        piece = out[:, hf * LANES:(hf + 1) * LANES]
        if transpose_out:
            o_ref[0, hf] = piece.T[0:NSA_HEAD_DIM, :].astype(o_ref.dtype)
        else:
            o_ref[0, hf] = piece.astype(o_ref.dtype)


def _compress(kv_in, pe, w1, b1, w2, *, transpose_out):
    bsz, s, _ = kv_in.shape
    dk, hid = NSA_HEAD_DIM, w1.shape[1]
    nb = s // CMP_STRIDE
    w1r = w1.reshape(CMP_BLOCK, dk, hid)
    z = jnp.zeros_like(w1r)
    w1p = jnp.concatenate([jnp.concatenate([w1r, z], axis=2), jnp.concatenate([z, w1r], axis=2)],
                          axis=1).astype(BF16)
    w2w = jnp.pad(w2, ((0, 0), (0, LANES - dk)))
    z2 = jnp.zeros_like(w2w)
    w2p = jnp.concatenate([jnp.concatenate([w2w, z2], axis=1), jnp.concatenate([z2, w2w], axis=1)],
                          axis=0).astype(BF16)
    const2 = lambda b, p: (0, 0)
    out_block = (1, 2, dk, nb) if transpose_out else (1, 2, nb, LANES)
    out_shape = (bsz, NSA_GROUPS, dk, nb) if transpose_out else (bsz, NSA_GROUPS, nb, LANES)
    return pl.pallas_call(
        functools.partial(_compress_kernel, transpose_out=transpose_out),
        grid=(bsz, NSA_GROUPS // 2),
        in_specs=[pl.BlockSpec((1, s, LANES), lambda b, p: (b, 0, p)),
                  pl.BlockSpec((CMP_BLOCK, LANES), const2),
                  _resident((CMP_BLOCK, LANES, 2 * hid), lambda b, p: (0, 0, 0)),
                  pl.BlockSpec((1, 2 * hid), const2),
                  _resident((2 * hid, 2 * LANES), const2)],
        out_specs=pl.BlockSpec(out_block, lambda b, p: (b, p, 0, 0)),
        out_shape=jax.ShapeDtypeStruct(out_shape, BF16),
        compiler_params=_params(2),
        name="nsa_compress",
    )(kv_in, jnp.concatenate([pe, pe], axis=1), w1p, jnp.concatenate([b1, b1]).reshape(1, 2 * hid), w2p)


def _nsa_kernel(bg_ref, q_ref, gt_ref, kc_ref, vc_ref, ks_ref, vs_ref, kw_ref, vw_ref, msel_ref,
                o_ref, qa_sc, s_sc, bias_sc, m_sc, acc_sc, part_sc):
    gp = pl.program_id(1)
    qi = pl.program_id(2)
    tq = q_ref.shape[0]
    rep = NSA_HEADS // NSA_GROUPS
    dk = NSA_HEAD_DIM
    half = LANES // 2
    n_sel = 32
    units = range(NSA_STEP_GROUPS)
    heads = range(rep)
    s0 = qi * tq
    key_off = lax.broadcasted_iota(jnp.int32, (tq, tq), 0)
    qry_off = lax.broadcasted_iota(jnp.int32, (tq, tq), 1)
    cols = lambda r: slice(r * tq, (r + 1) * tq)
    nt_dot = lambda k, q: lax.dot_general(k, q, NT_DIMS, preferred_element_type=F32)
    q_head = lambda u, r: q_ref[:, (u * rep + r) * LANES:(u * rep + r + 1) * LANES]

    def gate_row(u, r, br):
        idx = (gp * NSA_STEP_GROUPS + u) * (rep * NSA_BRANCHES) + r * NSA_BRANCHES + br
        return jax.nn.sigmoid(gt_ref[0, pl.ds(idx, 1), :] + bg_ref[idx])

    kp = jnp.maximum(qi - 1, 0)
    p0 = pl.multiple_of(kp * tq, tq)
    d0 = pl.multiple_of(qi * tq, tq)
    cmp_scores = [[nt_dot(kc_ref[0, u], q_head(u, r)) for r in heads] for u in units]
    win_scores = [[(nt_dot(kw_ref[0, u, pl.ds(p0, tq), :], q_head(u, r)),
                    nt_dot(kw_ref[0, u, pl.ds(d0, tq), :], q_head(u, r))) for r in heads] for u in units]
    for u in units:
        for r in heads:
            s_sc[1, u, r] = nt_dot(ks_ref[0, u, pl.ds(d0, tq), :], q_head(u, r))

    n_blk = kc_ref.shape[2]
    blk_id = lax.broadcasted_iota(jnp.int32, (n_blk, tq), 0)
    t_blk = s0 + lax.broadcasted_iota(jnp.int32, (n_blk, tq), 1)
    cmask = (blk_id * CMP_STRIDE + (CMP_BLOCK - 1) <= t_blk) & (blk_id < n_blk - 1)
    psum = [None] * NSA_STEP_GROUPS
    for r in heads:
        for u in units:
            scm = jnp.where(cmask, cmp_scores[u][r], NEG)
            e = jnp.where(cmask, jnp.exp2(scm - jnp.max(scm, axis=0, keepdims=True)), 0.0)
            den = jnp.sum(e, axis=0, keepdims=True)
            p = e * (1.0 / jnp.where(den > 0.0, den, 1.0))
            psum[u] = p if psum[u] is None else psum[u] + p
            part_sc[u, :, cols(r)] = gate_row(u, r, 0) * jnp.dot(vc_ref[0, u], p.astype(BF16),
                                                                 preferred_element_type=F32)

    jb = lax.broadcasted_iota(jnp.int32, (n_sel, tq), 0)
    jbf = jb.astype(F32)
    cur = (s0 + lax.broadcasted_iota(jnp.int32, (n_sel, tq), 1)) // SEL_BLOCK
    valid = jb <= cur
    forced = ((jb == 0) | (jb == cur) | (jb == cur - 1)) & valid
    for u in units:
        p_hi = psum[u].astype(BF16)
        p_lo = (psum[u] - p_hi.astype(F32)).astype(BF16)
        score_t = (jnp.dot(msel_ref[...], p_hi, preferred_element_type=F32)
                   + jnp.dot(msel_ref[...], p_lo, preferred_element_type=F32))
        s2 = jnp.where(forced, FORCED_SCORE, jnp.where(valid, score_t[half:half + n_sel, :], -1.0))
        for _ in range(SEL_TOPN):
            best = jnp.max(s2, axis=0, keepdims=True)
            first = jnp.min(jnp.where(s2 == best, jbf, float(n_sel)), axis=0, keepdims=True)
            s2 = jnp.where(jbf == first, TAKEN, s2)
        bias_t = jnp.where((s2 == TAKEN) & valid, 0.0, NEG)
        bias_sc[u] = bias_t
        bias_full = jnp.concatenate(
            [jnp.zeros((half, tq), F32), bias_t, jnp.zeros((LANES - half - n_sel, tq), F32)], axis=0)
        bias_q = bias_full.T
        for r in heads:
            qa_sc[u, r] = (q_head(u, r).astype(F32) + bias_q).astype(BF16)

    prev_mask = (key_off > qry_off) & (qi >= 1)
    diag_mask = key_off <= qry_off
    for r in heads:
        for u in units:
            s_prev = jnp.where(prev_mask, win_scores[u][r][0], NEG)
            s_diag = jnp.where(diag_mask, win_scores[u][r][1], NEG)
            mw = jnp.maximum(jnp.max(s_prev, axis=0, keepdims=True), jnp.max(s_diag, axis=0, keepdims=True))
            win = (jnp.dot(vw_ref[0, u, kp], jnp.exp2(s_prev - mw).astype(BF16), preferred_element_type=F32)
                   + jnp.dot(vw_ref[0, u, qi], jnp.exp2(s_diag - mw).astype(BF16),
                             preferred_element_type=F32))
            part_sc[u, :, cols(r)] += gate_row(u, r, 2) * (win[0:dk] * (1.0 / win[dk:dk + 1]))

    blk0 = qi * (tq // SEL_BLOCK)
    for u in units:
        bias_diag = jnp.concatenate(
            [jnp.broadcast_to(bias_sc[u, pl.ds(blk0 + jj, 1), :], (SEL_BLOCK, tq))
             for jj in range(tq // SEL_BLOCK)], axis=0)
        for r in heads:
            s = jnp.where(diag_mask, s_sc[1, u, r] + bias_diag, NEG)
            m0 = jnp.max(s, axis=0, keepdims=True)
            m_sc[u, :, cols(r)] = m0
            acc_sc[u, :, cols(r)] = jnp.dot(vs_ref[0, u, qi], jnp.exp2(s - m0).astype(BF16),
                                            preferred_element_type=F32)

    def qk_into(kt, slot):
        k0 = pl.multiple_of(kt * tq, tq)
        for u in units:
            k_tile = ks_ref[0, u, pl.ds(k0, tq), :]
            for r in heads:
                s_sc[slot, u, r] = nt_dot(k_tile, qa_sc[u, r])

    def softmax_pv(kt, slot):
        for u in units:
            v_t = vs_ref[0, u, kt]
            for r in heads:
                s = s_sc[slot, u, r]
                m_old = m_sc[u, :, cols(r)]
                m_new = jnp.maximum(m_old, jnp.max(s, axis=0, keepdims=True))
                alpha = jnp.exp2(m_old - m_new)
                p = jnp.exp2(s - m_new)
                acc_sc[u, :, cols(r)] = alpha * acc_sc[u, :, cols(r)] + jnp.dot(
                    v_t, p.astype(BF16), preferred_element_type=F32)
                m_sc[u, :, cols(r)] = m_new

    qk_into(0, 0)

    def pair_body(j, carry):
        a = 2 * j
        qk_into(a + 1, 1)
        softmax_pv(a, 0)
        qk_into(a + 2, 0)
        softmax_pv(a + 1, 1)
        return carry

    lax.fori_loop(0, qi // 2, pair_body, 0)

    @pl.when(qi % 2 == 1)
    def _():
        softmax_pv(qi - 1, 0)

    for u in units:
        merged = []
        for r in heads:
            o_slc = acc_sc[u, 0:dk, cols(r)] * (1.0 / acc_sc[u, dk:dk + 1, cols(r)])
            merged.append(part_sc[u, :, cols(r)] + gate_row(u, r, 1) * o_slc)
        for pair in range(rep // 2):
            both = jnp.concatenate([merged[2 * pair], merged[2 * pair + 1]], axis=0)
            c0 = (u * (rep // 2) + pair) * LANES
            o_ref[:, c0:c0 + LANES] = both.T.astype(o_ref.dtype)


def _nsa_attention(q, gt, b_gate, kc, vc_t, ks, vs_t, kw, vw_t, msel_t, bsz, s):
    n = q.shape[0]
    tq = NSA_QTILE
    assert tq == WINDOW and s % tq == 0 and s // SEL_BLOCK == 32
    rep = NSA_HEADS // NSA_GROUPS
    dk = NSA_HEAD_DIM
    nq = s // tq
    ug = NSA_STEP_GROUPS
    per_bg = lambda a: pl.BlockSpec((1, ug) + a.shape[2:], lambda b, g, i: (b, g) + (0,) * (a.ndim - 2))
    return pl.pallas_call(
        _nsa_kernel,
        grid=(bsz, NSA_GROUPS // ug, nq),
        in_specs=[pl.BlockSpec(memory_space=pltpu.SMEM),
                  pl.BlockSpec((tq, ug * rep * LANES), lambda b, g, i: (b * nq + i, g)),
                  pl.BlockSpec((1, LANES, tq), lambda b, g, i: (b, 0, i)),
                  per_bg(kc), per_bg(vc_t), per_bg(ks), per_bg(vs_t), per_bg(kw), per_bg(vw_t),
                  pl.BlockSpec(msel_t.shape, lambda b, g, i: (0, 0))],
        out_specs=pl.BlockSpec((tq, ug * rep * dk), lambda b, g, i: (b * nq + i, g)),
        out_shape=jax.ShapeDtypeStruct((n, NSA_HEADS * dk), BF16),
        scratch_shapes=[pltpu.VMEM((ug, rep, tq, LANES), BF16),
                        pltpu.VMEM((2, ug, rep, tq, tq), F32),
                        pltpu.VMEM((ug, s // SEL_BLOCK, tq), F32),
                        pltpu.VMEM((ug, 1, rep * tq), F32),
                        pltpu.VMEM((ug, NSA_VROWS, rep * tq), F32),
                        pltpu.VMEM((ug, dk, rep * tq), F32)],
        compiler_params=_params(3),
        name="nsa_attention",
    )(b_gate, q, gt, kc, vc_t, ks, vs_t, kw, vw_t, msel_t)


def _sel_matrix_t(n_rows, n_sel):
    n_cmp = np.arange(n_rows)[None, :] * CMP_STRIDE
    sel = np.arange(n_sel)[:, None] * SEL_BLOCK
    ov = np.clip(np.minimum(n_cmp + CMP_BLOCK, sel + SEL_BLOCK) - np.maximum(n_cmp, sel), 0, None)
    out = np.zeros((LANES, n_rows), np.float32)
    out[LANES // 2:LANES // 2 + n_sel] = ov / CMP_BLOCK
    return jnp.asarray(out, dtype=BF16)


def _nsa_mixer(x, g, w_in, pe_k, k_w1, k_b1, k_w2, pe_v, v_w1, v_b1, v_w2, b_gate, w_out, bsz, s):
    n_gate = NSA_BRANCHES * NSA_HEADS
    kv = NSA_GROUPS * NSA_HEAD_DIM
    w_in_p = jnp.pad(w_in, ((0, 0), (0, LANES - n_gate))).astype(BF16)
    q, gt, kc_in, vc_in, ks, kw, vs_t, vw_t = _nsa_proj(x, g, w_in_p, bsz, s)
    kc = _compress(kc_in.reshape(bsz, s, kv), pe_k, k_w1, k_b1, k_w2, transpose_out=False)
    vc_t = _compress(vc_in.reshape(bsz, s, kv), pe_v, v_w1, v_b1, v_w2, transpose_out=True)
    msel_t = _sel_matrix_t(s // CMP_STRIDE, s // SEL_BLOCK)
    attn = _nsa_attention(q, gt, b_gate, kc, vc_t, ks, vs_t, kw, vw_t, msel_t, bsz, s)
    return [(attn, w_out.astype(BF16))]


def kernel(x, ffn1_norm, ffn1_w_gu, ffn1_w_down, mix_norm, ffn2_norm, ffn2_w_gu, ffn2_w_down,
           ab_w_in, rg_conv_w, rg_conv_b, rg_w_r, rg_b_r, rg_w_i, rg_b_i, rg_lambda,
           ml_conv_w, ml_conv_b, ml_b_i, ml_b_f, ml_norm, ab_w_out,
           nsa_w_in, nsa_pe_k, nsa_k_w1, nsa_k_b1, nsa_k_w2, nsa_pe_v, nsa_v_w1, nsa_v_b1, nsa_v_w2,
           nsa_b_gate, nsa_w_out, final_norm):
    bsz, s, d = x.shape
    depth = ffn1_norm.shape[0]
    h = x.reshape(bsz * s, d)
    for i in range(depth):
        j = i // 2
        h = _ffn(h, [], ffn1_norm[i], ffn1_w_gu[i].astype(BF16), ffn1_w_down[i].astype(BF16),
                 final_norm, final=False)
        if i % 2 == 0:
            mix = _ab_mixer(h, mix_norm[i], ab_w_in[j], rg_conv_w[j], rg_conv_b[j], rg_w_r[j], rg_b_r[j],
                            rg_w_i[j], rg_b_i[j], rg_lambda[j], ml_conv_w[j], ml_conv_b[j], ml_b_i[j],
                            ml_b_f[j], ml_norm[j], ab_w_out[j], bsz, s)
        else:
            mix = _nsa_mixer(h, mix_norm[i], nsa_w_in[j], nsa_pe_k[j], nsa_k_w1[j], nsa_k_b1[j],
                             nsa_k_w2[j], nsa_pe_v[j], nsa_v_w1[j], nsa_v_b1[j], nsa_v_w2[j],
                             nsa_b_gate[j], nsa_w_out[j], bsz, s)
        h = _ffn(h, mix, ffn2_norm[i], ffn2_w_gu[i].astype(BF16), ffn2_w_down[i].astype(BF16),
                 final_norm, final=(i == depth - 1))
    return h.reshape(bsz, s, d)
```

```python
import functools

import jax
import jax.numpy as jnp
import numpy as np
from jax import lax
from jax.experimental import pallas as pl
from jax.experimental.pallas import tpu as pltpu

F32 = jnp.float32
BF16 = jnp.bfloat16
HIGHEST = lax.Precision.HIGHEST

EPS = 1e-6
RG_C = 8.0
RG_BLOCKS = 8
CONV_WIDTH = 4
ML_HEADS = 4
ML_CHUNK = 128
NSA_HEADS = 16
NSA_GROUPS = 4
NSA_HEAD_DIM = 64
NSA_BRANCHES = 3
CMP_BLOCK = 32
CMP_STRIDE = 16
SEL_BLOCK = 64
SEL_TOPN = 8
WINDOW = 256
FORCED_SCORE = 1e6

LANES = 128
SUBLANES = 8
MXU_TILE = 256
NEG = -1e30
TAKEN = -2.0
VMEM_LIMIT = 48 * 1024 * 1024

TOKEN_TILE = 512
NSA_QTILE = 256
NSA_VROWS = 80
NSA_STEP_GROUPS = 4
LOG2E = 1.4426950408889634
NT_DIMS = (((1,), (1,)), ((), ()))


def _params(n_axes):
    return pltpu.CompilerParams(dimension_semantics=("arbitrary",) * n_axes,
                                vmem_limit_bytes=VMEM_LIMIT)


def _resident(shape, index_map):
    return pl.BlockSpec(shape, index_map, pipeline_mode=pl.Buffered(1))


def _rms(x, g):
    return x * lax.rsqrt(jnp.mean(x * x, axis=-1, keepdims=True) + EPS) * g


def _gelu_tanh(x):
    return 0.5 * x * (1.0 + jnp.tanh(0.7978845608028654 * (x + 0.044715 * (x * x * x))))


def _softplus(z):
    return jnp.maximum(z, 0.0) + jnp.log1p(jnp.exp(-jnp.abs(z)))


def _sigmoid(z):
    return 0.5 * jnp.tanh(0.5 * z) + 0.5


def _sqrt_nonneg(z):
    return jnp.where(z > 0.0, z * lax.rsqrt(z), 0.0)


def _ffn_kernel(*refs, n_mix, n_chunks, final):
    x_ref = refs[0]
    a_refs = refs[1:1 + n_mix]
    wo_refs = refs[1 + n_mix:1 + 2 * n_mix]
    g_ref, wg_ref, wu_ref, wd_ref, gf_ref, o_ref = refs[1 + 2 * n_mix:]
    x = x_ref[...]
    for a_ref, wo_ref in zip(a_refs, wo_refs):
        x = x + jnp.dot(a_ref[...], wo_ref[...], preferred_element_type=F32)
    xn = _rms(x, g_ref[...]).astype(BF16)
    n_tiles = wg_ref.shape[1] // MXU_TILE
    bounds = [MXU_TILE * ((n_tiles * c + n_chunks - 1) // n_chunks) for c in range(n_chunks + 1)]
    acc = None
    for lo, hi in zip(bounds[:-1], bounds[1:]):
        gate = jnp.dot(xn, wg_ref[:, lo:hi], preferred_element_type=F32)
        up = jnp.dot(xn, wu_ref[:, lo:hi], preferred_element_type=F32)
        h = (gate * jax.nn.sigmoid(gate) * up).astype(BF16)
        part = jnp.dot(h, wd_ref[lo:hi, :], preferred_element_type=F32)
        acc = part if acc is None else acc + part
    y = x + 0.5 * acc
    if final:
        y = _rms(y, gf_ref[...])
    o_ref[...] = y


def _ffn(x, mix, g, w_gu, w_down, gf, *, layer, final):
    n, d = x.shape
    f = w_down.shape[1]
    tm = min(TOKEN_TILE, n)
    const = lambda i: (0, 0)
    row = lambda i: (i, 0)
    acts = [a for a, _ in mix]
    wos = [w for _, w in mix]
    return pl.pallas_call(
        functools.partial(_ffn_kernel, n_mix=len(mix), n_chunks=2, final=final),
        grid=(n // tm,),
        in_specs=[pl.BlockSpec((tm, d), row)]
        + [pl.BlockSpec((tm, a.shape[1]), row) for a in acts]
        + [_resident(w.shape, const) for w in wos]
        + [pl.BlockSpec((1, d), const),
           _resident((None, d, f), lambda i: (layer, 0, 0)),
           _resident((None, d, f), lambda i: (layer, 0, 1)),
           _resident((None, f, d), lambda i: (layer, 0, 0)),
           pl.BlockSpec((1, d), const)],
        out_specs=pl.BlockSpec((tm, d), row),
        out_shape=jax.ShapeDtypeStruct((n, d), F32),
        compiler_params=_params(1),
        name="ffn_final" if final else ("ffn_mix" if mix else "ffn"),
    )(x, *acts, *wos, g.reshape(1, d), w_gu, w_gu, w_down, gf.reshape(1, d))


def _norm_matmul_kernel(x_ref, g_ref, w_ref, o_ref):
    xn = _rms(x_ref[...], g_ref[...]).astype(BF16)
    o_ref[...] = jnp.dot(xn, w_ref[...], preferred_element_type=F32)


def _norm_matmul(x, g, w):
    n, d = x.shape
    dout = w.shape[1]
    tm = min(TOKEN_TILE, n)
    const = lambda i: (0, 0)
    return pl.pallas_call(
        _norm_matmul_kernel,
        grid=(n // tm,),
        in_specs=[pl.BlockSpec((tm, d), lambda i: (i, 0)),
                  pl.BlockSpec((1, d), const),
                  _resident((d, dout), const)],
        out_specs=pl.BlockSpec((tm, dout), lambda i: (i, 0)),
        out_shape=jax.ShapeDtypeStruct((n, dout), F32),
        compiler_params=_params(1),
        name="norm_matmul",
    )(x, g.reshape(1, d), w)


def _causal_conv(x, cw, cb, pad_ref):
    s = x.shape[0]
    pad_ref[0:SUBLANES, :] = jnp.zeros((SUBLANES, x.shape[1]), F32)
    pad_ref[SUBLANES:SUBLANES + s, :] = x
    y = cb + cw[CONV_WIDTH - 1:CONV_WIDTH, :] * x
    for j in range(CONV_WIDTH - 1):
        off = SUBLANES - (CONV_WIDTH - 1 - j)
        y = y + cw[j:j + 1, :] * pad_ref[off:off + s, :]
    return y


def _causal_conv_blocks(x, cw, cb, pad_ref, emit, block=256):
    s = x.shape[0]
    pad_ref[0:SUBLANES, :] = jnp.zeros((SUBLANES, x.shape[1]), F32)
    pad_ref[SUBLANES:SUBLANES + s, :] = x
    for r0 in range(0, s, block):
        y = cb + cw[CONV_WIDTH - 1:CONV_WIDTH, :] * pad_ref[SUBLANES + r0:SUBLANES + r0 + block, :]
        for j in range(CONV_WIDTH - 1):
            off = r0 + SUBLANES - (CONV_WIDTH - 1 - j)
            y = y + cw[j:j + 1, :] * pad_ref[off:off + block, :]
        emit(r0, y)


def _rglru_kernel(xa_ref, ga_ref, cw_ref, cb_ref, wr_ref, br_ref, wi_ref, bi_ref, lam_ref,
                  o_ref, pad_ref, a_ref, u_ref):
    s = xa_ref.shape[1]
    ng = s // SUBLANES
    xc = _causal_conv(xa_ref[0], cw_ref[...], cb_ref[...], pad_ref)
    xb = xc.astype(BF16)
    r = _sigmoid(jnp.dot(xb, wr_ref[...], preferred_element_type=F32) + br_ref[...])
    ig = _sigmoid(jnp.dot(xb, wi_ref[...], preferred_element_type=F32) + bi_ref[...])
    log_a = -RG_C * r * _softplus(-lam_ref[...])
    a = jnp.exp(log_a)
    th = jnp.tanh(log_a)
    u = _sqrt_nonneg(-2.0 * th / (1.0 - th)) * (ig * xc)
    a3 = a.reshape(ng, SUBLANES, LANES)
    u3 = u.reshape(ng, SUBLANES, LANES)
    row = lax.broadcasted_iota(jnp.int32, (ng, SUBLANES, LANES), 1)
    sh = 1
    while sh < SUBLANES:
        a_s = pltpu.roll(a3, sh, 1)
        u_s = pltpu.roll(u3, sh, 1)
        m = row >= sh
        u3 = jnp.where(m, a3 * u_s + u3, u3)
        a3 = jnp.where(m, a3 * a_s, a3)
        sh *= 2
    a_ref[...] = a3
    u_ref[...] = u3

    def body(i, h):
        hh = a_ref[i] * h + u_ref[i]
        u_ref[i] = hh
        return hh[SUBLANES - 1:SUBLANES, :]

    lax.fori_loop(0, ng, body, jnp.zeros((1, LANES), F32), unroll=8)
    o_ref[0] = (_gelu_tanh(ga_ref[0]) * u_ref[...].reshape(s, LANES)).astype(o_ref.dtype)


def _rglru(proj3, cw, cb, wr_bd, br, wi_bd, bi, lam):
    bsz, s, _ = proj3.shape
    c = cw.shape[1]
    nb = c // LANES
    vec = lambda b, j: (0, j)
    return pl.pallas_call(
        _rglru_kernel,
        grid=(bsz, nb),
        in_specs=[pl.BlockSpec((1, s, LANES), lambda b, j: (b, 0, j)),
                  pl.BlockSpec((1, s, LANES), lambda b, j: (b, 0, nb + j)),
                  pl.BlockSpec((CONV_WIDTH, LANES), vec),
                  pl.BlockSpec((1, LANES), vec),
                  pl.BlockSpec((LANES, LANES), lambda b, j: (j, j)),
                  pl.BlockSpec((1, LANES), vec),
                  pl.BlockSpec((LANES, LANES), lambda b, j: (j, j)),
                  pl.BlockSpec((1, LANES), vec),
                  pl.BlockSpec((1, LANES), vec)],
        out_specs=pl.BlockSpec((1, s, LANES), lambda b, j: (b, 0, j)),
        out_shape=jax.ShapeDtypeStruct((bsz, s, c), BF16),
        scratch_shapes=[pltpu.VMEM((s + SUBLANES, LANES), F32),
                        pltpu.VMEM((s // SUBLANES, SUBLANES, LANES), F32),
                        pltpu.VMEM((s // SUBLANES, SUBLANES, LANES), F32)],
        compiler_params=_params(2),
        name="rglru",
    )(proj3, proj3, cw, cb.reshape(1, c), wr_bd, br.reshape(1, c), wi_bd, bi.reshape(1, c),
      lam.reshape(1, c))


def _mlstm_kernel(bias_ref, q_ref, k_ref, v_ref, og_ref, gi_ref, gf_ref, cwq_ref, cbq_ref,
                  cwk_ref, cbk_ref, ng_ref, o_ref, pad_ref, q_sc, k_sc, li_sc, lf_sc, bc_sc):
    hd = pl.program_id(1)
    s = q_ref.shape[1]
    dh = q_ref.shape[2]
    nc = s // ML_CHUNK

    def put_q(r0, y):
        q_sc[r0:r0 + y.shape[0], :] = jax.nn.silu(y) * (dh ** -0.5)

    def put_k(r0, y):
        k_sc[r0:r0 + y.shape[0], :] = jax.nn.silu(y)

    _causal_conv_blocks(q_ref[0], cwq_ref[...], cbq_ref[...], pad_ref, put_q)
    _causal_conv_blocks(k_ref[0], cwk_ref[...], cbk_ref[...], pad_ref, put_k)
    li_sc[...] = gi_ref[0, 0] + bias_ref[hd]
    lf = -_softplus(-(gf_ref[0, 0] + bias_ref[ML_HEADS + hd]))
    lf_sc[...] = lf
    jj = lax.broadcasted_iota(jnp.int32, (ML_CHUNK, ML_CHUNK), 0)
    kk = lax.broadcasted_iota(jnp.int32, (ML_CHUNK, ML_CHUNK), 1)
    tri = kk <= jj
    upper = jnp.where(jj <= kk, 1.0, 0.0).astype(F32)
    bc_sc[...] = jnp.dot(lf, upper, precision=HIGHEST, preferred_element_type=F32)
    norm_g = ng_ref[...]

    def chunk(c, carry):
        c_st, n_st, m_st = carry
        r0 = pl.multiple_of(c * ML_CHUNK, ML_CHUNK)
        qc = q_sc[pl.ds(r0, ML_CHUNK), :]
        kc = k_sc[pl.ds(r0, ML_CHUNK), :]
        vc = v_ref[0, pl.ds(r0, ML_CHUNK), :]
        li_row = li_sc[pl.ds(c, 1), :]
        lf_row = lf_sc[pl.ds(c, 1), :]
        b_row = bc_sc[pl.ds(c, 1), :]
        b_col = jnp.sum(jnp.where(tri, lf_row, 0.0), axis=1, keepdims=True)
        g_tot = jnp.sum(lf_row, axis=1, keepdims=True)
        w_row = g_tot - b_row + li_row
        m_loc = jnp.max(w_row, axis=1, keepdims=True)
        wk = jnp.exp(w_row - m_loc)
        lhs = jnp.concatenate([vc.T * wk, jnp.broadcast_to(wk, (2 * SUBLANES, ML_CHUNK))], axis=0)
        cn = jnp.dot(lhs.astype(BF16), kc.astype(BF16), preferred_element_type=F32)
        c_loc = cn[0:dh]
        n_loc = cn[dh:dh + 1]
        d = jnp.where(tri, b_col - b_row + li_row, NEG)
        m_inter = b_col + m_st
        m = jnp.maximum(m_inter, jnp.max(d, axis=1, keepdims=True))
        qb = qc.astype(BF16)
        qk = lax.dot_general(qb, kc.astype(BF16), NT_DIMS, preferred_element_type=F32)
        p = jnp.exp(d - m) * qk
        sc = jnp.exp(m_inter - m)
        inter = lax.dot_general(qb, c_st.astype(BF16), NT_DIMS, preferred_element_type=F32)
        num = sc * inter + jnp.dot(p.astype(BF16), vc.astype(BF16), preferred_element_type=F32)
        den = jnp.sum(sc * (qc * n_st) + p, axis=1, keepdims=True)
        h = num / jnp.maximum(jnp.abs(den), jnp.exp(-m))
        h = h * lax.rsqrt(jnp.mean(h * h, axis=1, keepdims=True) + EPS) * norm_g
        o_ref[0, pl.ds(r0, ML_CHUNK), :] = (
            jax.nn.sigmoid(og_ref[0, pl.ds(r0, ML_CHUNK), :]) * h).astype(o_ref.dtype)
        m_new = jnp.maximum(g_tot + m_st, m_loc)
        sa = jnp.exp(g_tot + m_st - m_new)
        sb = jnp.exp(m_loc - m_new)
        return sa * c_st + sb * c_loc, sa * n_st + sb * n_loc, m_new

    init = (jnp.zeros((dh, dh), F32), jnp.zeros((1, dh), F32), jnp.full((1, 1), NEG, F32))
    lax.fori_loop(0, nc, chunk, init, unroll=4)


def _mlstm(proj3, gates_t, bias, cw, cb, norm_g, *, col0):
    bsz, s, _ = proj3.shape
    nh = ML_HEADS
    dh = LANES
    nc = s // ML_CHUNK
    base = col0 // dh
    blk = lambda off: pl.BlockSpec((1, s, dh), lambda b, h: (b, 0, base + off + h))
    vec = lambda off: (lambda b, h: (0, off + h))
    return pl.pallas_call(
        _mlstm_kernel,
        grid=(bsz, nh),
        in_specs=[pl.BlockSpec(memory_space=pltpu.SMEM),
                  blk(0), blk(nh), blk(2 * nh), blk(3 * nh),
                  pl.BlockSpec((1, 1, nc, ML_CHUNK), lambda b, h: (b, h, 0, 0)),
                  pl.BlockSpec((1, 1, nc, ML_CHUNK), lambda b, h: (b, nh + h, 0, 0)),
                  pl.BlockSpec((CONV_WIDTH, dh), vec(0)), pl.BlockSpec((1, dh), vec(0)),
                  pl.BlockSpec((CONV_WIDTH, dh), vec(nh)), pl.BlockSpec((1, dh), vec(nh)),
                  pl.BlockSpec((1, dh), vec(0))],
        out_specs=pl.BlockSpec((1, s, dh), lambda b, h: (b, 0, h)),
        out_shape=jax.ShapeDtypeStruct((bsz, s, nh * dh), BF16),
        scratch_shapes=[pltpu.VMEM((s + SUBLANES, dh), F32),
                        pltpu.VMEM((s, dh), F32), pltpu.VMEM((s, dh), F32),
                        pltpu.VMEM((nc, ML_CHUNK), F32), pltpu.VMEM((nc, ML_CHUNK), F32),
                        pltpu.VMEM((nc, ML_CHUNK), F32)],
        compiler_params=_params(2),
        name="mlstm",
    )(bias, proj3, proj3, proj3, proj3, gates_t, gates_t, cw, cb.reshape(1, -1), cw,
      cb.reshape(1, -1), norm_g.reshape(1, -1))


def _ab_mixer(x, g, w_in, rg_conv_w, rg_conv_b, rg_w_r, rg_b_r, rg_w_i, rg_b_i, rg_lambda,
              ml_conv_w, ml_conv_b, ml_b_i, ml_b_f, ml_norm, w_out, bsz, s):
    d_rg = rg_conv_w.shape[1]
    d_ml = ml_norm.shape[0]
    d_main = 2 * d_rg + 4 * d_ml
    n_gate = 2 * ML_HEADS
    w_in_p = jnp.pad(w_in, ((0, 0), (0, LANES - n_gate))).astype(BF16)
    proj = _norm_matmul(x, g, w_in_p)
    proj3 = proj.reshape(bsz, s, proj.shape[1])
    bd = lambda w: jax.scipy.linalg.block_diag(*[w[i] for i in range(RG_BLOCKS)]).astype(BF16)
    ya = _rglru(proj3, rg_conv_w, rg_conv_b, bd(rg_w_r), rg_b_r, bd(rg_w_i), rg_b_i, rg_lambda)
    gates_t = proj3[:, :, d_main:d_main + n_gate].transpose(0, 2, 1).reshape(
        bsz, n_gate, s // ML_CHUNK, ML_CHUNK)
    yb = _mlstm(proj3, gates_t, jnp.concatenate([ml_b_i, ml_b_f]), ml_conv_w, ml_conv_b, ml_norm,
                col0=2 * d_rg)
    w_out_b = w_out.astype(BF16)
    return [(ya.reshape(bsz * s, d_rg), w_out_b[:d_rg]), (yb.reshape(bsz * s, d_ml), w_out_b[d_rg:])]


def _nsa_proj_kernel(x_ref, g_ref, w_ref, q_ref, gt_ref, kc_ref, vc_ref, ks_ref, kw_ref, vs_ref, vw_ref,
                     *, seq):
    tm = x_ref.shape[0]
    kv = NSA_GROUPS * NSA_HEAD_DIM
    width = NSA_HEADS * NSA_HEAD_DIM
    half = LANES // 2
    xn = _rms(x_ref[...], g_ref[...]).astype(BF16)
    proj = jnp.dot(xn, w_ref[...], preferred_element_type=F32)
    lane = lax.broadcasted_iota(jnp.int32, (tm, LANES), 1)
    lo = lane < half

    def lane_pair(c0):
        p = proj[:, c0:c0 + LANES]
        return p, pltpu.roll(p, half, 1)

    scale = NSA_HEAD_DIM ** -0.5 * LOG2E
    for pair in range(NSA_HEADS // 2):
        for j, piece in enumerate(lane_pair(pair * LANES)):
            h = 2 * pair + j
            q_ref[:, h * LANES:(h + 1) * LANES] = jnp.where(lo, piece * scale, 0.0).astype(BF16)
    gt_ref[0] = proj[:, width + 6 * kv:width + 6 * kv + LANES].T
    kc_ref[...] = proj[:, width:width + kv]
    vc_ref[...] = proj[:, width + kv:width + 2 * kv]
    pos = (pl.program_id(0) % (seq // tm)) * tm + lax.broadcasted_iota(jnp.int32, (tm, LANES), 0)
    onehot = jnp.where(lane == half + pos // SEL_BLOCK, 1.0, 0.0)
    for pair in range(NSA_GROUPS // 2):
        for idx, ref, fill in ((2, ks_ref, onehot), (4, kw_ref, 0.0)):
            for j, piece in enumerate(lane_pair(width + idx * kv + pair * LANES)):
                ref[0, 2 * pair + j] = jnp.where(lo, piece, fill).astype(BF16)
        for idx, ref in ((3, vs_ref), (5, vw_ref)):
            c0 = width + idx * kv + pair * LANES
            t = proj[:, c0:c0 + LANES].T.astype(BF16)
            ones = jnp.ones((NSA_VROWS - half, NSA_QTILE), BF16)
            for j in range(tm // NSA_QTILE):
                for hf in range(2):
                    ref[0, 2 * pair + hf, j, 0:half, :] = t[hf * half:(hf + 1) * half,
                                                            j * NSA_QTILE:(j + 1) * NSA_QTILE]
                    ref[0, 2 * pair + hf, j, half:NSA_VROWS, :] = ones


def _nsa_proj(x, g, w, bsz, s):
    n, d = x.shape
    dout = w.shape[1]
    tm = min(TOKEN_TILE, s)
    nsb = s // tm
    ng, dk = NSA_GROUPS, NSA_HEAD_DIM
    kv = ng * dk
    tq = NSA_QTILE
    const = lambda i: (0, 0)
    row = lambda i: (i, 0)
    keyed = lambda i: (i // nsb, 0, i % nsb, 0)
    return pl.pallas_call(
        functools.partial(_nsa_proj_kernel, seq=s),
        grid=(n // tm,),
        in_specs=[pl.BlockSpec((tm, d), row), pl.BlockSpec((1, d), const), _resident((d, dout), const)],
        out_specs=[pl.BlockSpec((tm, NSA_HEADS * LANES), row),
                   pl.BlockSpec((1, LANES, tm), lambda i: (i // nsb, 0, i % nsb)),
                   pl.BlockSpec((tm, kv), row), pl.BlockSpec((tm, kv), row),
                   pl.BlockSpec((1, ng, tm, LANES), keyed), pl.BlockSpec((1, ng, tm, LANES), keyed),
                   pl.BlockSpec((1, ng, tm // tq, NSA_VROWS, tq), lambda i: (i // nsb, 0, i % nsb, 0, 0)),
                   pl.BlockSpec((1, ng, tm // tq, NSA_VROWS, tq), lambda i: (i // nsb, 0, i % nsb, 0, 0))],
        out_shape=[jax.ShapeDtypeStruct((n, NSA_HEADS * LANES), BF16),
                   jax.ShapeDtypeStruct((bsz, LANES, s), F32),
                   jax.ShapeDtypeStruct((n, kv), F32), jax.ShapeDtypeStruct((n, kv), F32),
                   jax.ShapeDtypeStruct((bsz, ng, s, LANES), BF16),
                   jax.ShapeDtypeStruct((bsz, ng, s, LANES), BF16),
                   jax.ShapeDtypeStruct((bsz, ng, s // tq, NSA_VROWS, tq), BF16),
                   jax.ShapeDtypeStruct((bsz, ng, s // tq, NSA_VROWS, tq), BF16)],
        compiler_params=_params(1),
        name="nsa_proj",
    )(x, g.reshape(1, d), w)


def _compress_kernel(x_ref, pe_ref, w1_ref, b1_ref, w2_ref, o_ref, *, transpose_out):
    nb = x_ref.shape[1] // CMP_STRIDE
    top = bot = None
    for l in range(CMP_STRIDE):
        rows = x_ref[0, pl.ds(l, nb, stride=CMP_STRIDE), :]
        t = jnp.dot((rows + pe_ref[l:l + 1, :]).astype(BF16), w1_ref[l], preferred_element_type=F32)
        b = jnp.dot((rows + pe_ref[CMP_STRIDE + l:CMP_STRIDE + l + 1, :]).astype(BF16),
                    w1_ref[CMP_STRIDE + l], preferred_element_type=F32)
        top = t if top is None else top + t
        bot = b if bot is None else bot + b
    pre = top + pltpu.roll(bot, nb - 1, 0) + b1_ref[...]
    out = jnp.dot(_gelu_tanh(pre).astype(BF16), w2_ref[...], preferred_element_type=F32)
    out = jnp.where(lax.broadcasted_iota(jnp.int32, out.shape, 0) < nb - 1, out, 0.0)
    for hf in range(2):
        piece = out[:, hf * LANES:(hf + 1) * LANES]
        if transpose_out:
            o_ref[0, hf] = piece.T[0:NSA_HEAD_DIM, :].astype(o_ref.dtype)
        else:
            o_ref[0, hf] = piece.astype(o_ref.dtype)


def _compress(kv_in, pe, w1, b1, w2, *, transpose_out):
    bsz, s, _ = kv_in.shape
    dk, hid = NSA_HEAD_DIM, w1.shape[1]
    nb = s // CMP_STRIDE
    w1r = w1.reshape(CMP_BLOCK, dk, hid)
    z = jnp.zeros_like(w1r)
    w1p = jnp.concatenate([jnp.concatenate([w1r, z], axis=2), jnp.concatenate([z, w1r], axis=2)],
                          axis=1).astype(BF16)
    w2w = jnp.pad(w2, ((0, 0), (0, LANES - dk)))
    z2 = jnp.zeros_like(w2w)
    w2p = jnp.concatenate([jnp.concatenate([w2w, z2], axis=1), jnp.concatenate([z2, w2w], axis=1)],
                          axis=0).astype(BF16)
    const2 = lambda b, p: (0, 0)
    out_block = (1, 2, dk, nb) if transpose_out else (1, 2, nb, LANES)
    out_shape = (bsz, NSA_GROUPS, dk, nb) if transpose_out else (bsz, NSA_GROUPS, nb, LANES)
    return pl.pallas_call(
        functools.partial(_compress_kernel, transpose_out=transpose_out),
        grid=(bsz, NSA_GROUPS // 2),
        in_specs=[pl.BlockSpec((1, s, LANES), lambda b, p: (b, 0, p)),
                  pl.BlockSpec((CMP_BLOCK, LANES), const2),
                  _resident((CMP_BLOCK, LANES, 2 * hid), lambda b, p: (0, 0, 0)),
                  pl.BlockSpec((1, 2 * hid), const2),
                  _resident((2 * hid, 2 * LANES), const2)],
        out_specs=pl.BlockSpec(out_block, lambda b, p: (b, p, 0, 0)),
        out_shape=jax.ShapeDtypeStruct(out_shape, BF16),
        compiler_params=_params(2),
        name="nsa_compress",
    )(kv_in, jnp.concatenate([pe, pe], axis=1), w1p, jnp.concatenate([b1, b1]).reshape(1, 2 * hid), w2p)


def _nsa_kernel(bg_ref, q_ref, gt_ref, kc_ref, vc_ref, ks_ref, vs_ref, kw_ref, vw_ref, msel_ref,
                o_ref, qa_sc, s_sc, bias_sc, m_sc, acc_sc, part_sc):
    gp = pl.program_id(1)
    qi = pl.program_id(2)
    tq = q_ref.shape[0]
    rep = NSA_HEADS // NSA_GROUPS
    dk = NSA_HEAD_DIM
    half = LANES // 2
    n_sel = 32
    units = range(NSA_STEP_GROUPS)
    heads = range(rep)
    s0 = qi * tq
    key_off = lax.broadcasted_iota(jnp.int32, (tq, tq), 0)
    qry_off = lax.broadcasted_iota(jnp.int32, (tq, tq), 1)
    cols = lambda r: slice(r * tq, (r + 1) * tq)
    nt_dot = lambda k, q: lax.dot_general(k, q, NT_DIMS, preferred_element_type=F32)
    q_head = lambda u, r: q_ref[:, (u * rep + r) * LANES:(u * rep + r + 1) * LANES]

    def gate_row(u, r, br):
        idx = (gp * NSA_STEP_GROUPS + u) * (rep * NSA_BRANCHES) + r * NSA_BRANCHES + br
        return jax.nn.sigmoid(gt_ref[0, pl.ds(idx, 1), :] + bg_ref[idx])

    kp = jnp.maximum(qi - 1, 0)
    p0 = pl.multiple_of(kp * tq, tq)
    d0 = pl.multiple_of(qi * tq, tq)
    cmp_scores = [[nt_dot(kc_ref[0, u], q_head(u, r)) for r in heads] for u in units]
    win_scores = [[(nt_dot(kw_ref[0, u, pl.ds(p0, tq), :], q_head(u, r)),
                    nt_dot(kw_ref[0, u, pl.ds(d0, tq), :], q_head(u, r))) for r in heads] for u in units]
    for u in units:
        for r in heads:
            s_sc[1, u, r] = nt_dot(ks_ref[0, u, pl.ds(d0, tq), :], q_head(u, r))

    n_blk = kc_ref.shape[2]
    blk_id = lax.broadcasted_iota(jnp.int32, (n_blk, tq), 0)
    t_blk = s0 + lax.broadcasted_iota(jnp.int32, (n_blk, tq), 1)
    cmask = (blk_id * CMP_STRIDE + (CMP_BLOCK - 1) <= t_blk) & (blk_id < n_blk - 1)
    psum = [None] * NSA_STEP_GROUPS
    for r in heads:
        for u in units:
            scm = jnp.where(cmask, cmp_scores[u][r], NEG)
            e = jnp.where(cmask, jnp.exp2(scm - jnp.max(scm, axis=0, keepdims=True)), 0.0)
            den = jnp.sum(e, axis=0, keepdims=True)
            p = e * (1.0 / jnp.where(den > 0.0, den, 1.0))
            psum[u] = p if psum[u] is None else psum[u] + p
            part_sc[u, :, cols(r)] = gate_row(u, r, 0) * jnp.dot(vc_ref[0, u], p.astype(BF16),
                                                                 preferred_element_type=F32)

    jb = lax.broadcasted_iota(jnp.int32, (n_sel, tq), 0)
    jbf = jb.astype(F32)
    cur = (s0 + lax.broadcasted_iota(jnp.int32, (n_sel, tq), 1)) // SEL_BLOCK
    valid = jb <= cur
    forced = ((jb == 0) | (jb == cur) | (jb == cur - 1)) & valid
    for u in units:
        p_hi = psum[u].astype(BF16)
        p_lo = (psum[u] - p_hi.astype(F32)).astype(BF16)
        score_t = (jnp.dot(msel_ref[...], p_hi, preferred_element_type=F32)
                   + jnp.dot(msel_ref[...], p_lo, preferred_element_type=F32))
        s2 = jnp.where(forced, FORCED_SCORE, jnp.where(valid, score_t[half:half + n_sel, :], -1.0))
        for _ in range(SEL_TOPN):
            best = jnp.max(s2, axis=0, keepdims=True)
            first = jnp.min(jnp.where(s2 == best, jbf, float(n_sel)), axis=0, keepdims=True)
            s2 = jnp.where(jbf == first, TAKEN, s2)
        bias_t = jnp.where((s2 == TAKEN) & valid, 0.0, NEG)
        bias_sc[u] = bias_t
        bias_full = jnp.concatenate(
            [jnp.zeros((half, tq), F32), bias_t, jnp.zeros((LANES - half - n_sel, tq), F32)], axis=0)
        bias_q = bias_full.T
        for r in heads:
            qa_sc[u, r] = (q_head(u, r).astype(F32) + bias_q).astype(BF16)

    prev_mask = (key_off > qry_off) & (qi >= 1)
    diag_mask = key_off <= qry_off
    for r in heads:
        for u in units:
            s_prev = jnp.where(prev_mask, win_scores[u][r][0], NEG)
            s_diag = jnp.where(diag_mask, win_scores[u][r][1], NEG)
            mw = jnp.maximum(jnp.max(s_prev, axis=0, keepdims=True), jnp.max(s_diag, axis=0, keepdims=True))
            win = (jnp.dot(vw_ref[0, u, kp], jnp.exp2(s_prev - mw).astype(BF16), preferred_element_type=F32)
                   + jnp.dot(vw_ref[0, u, qi], jnp.exp2(s_diag - mw).astype(BF16),
                             preferred_element_type=F32))
            part_sc[u, :, cols(r)] += gate_row(u, r, 2) * (win[0:dk] * (1.0 / win[dk:dk + 1]))

    def qk_into(kt, slot):
        k0 = pl.multiple_of(kt * tq, tq)
        for u in units:
            k_tile = ks_ref[0, u, pl.ds(k0, tq), :]
            for r in heads:
                s_sc[slot, u, r] = nt_dot(k_tile, qa_sc[u, r])

    qk_into(0, 0)
    blk0 = qi * (tq // SEL_BLOCK)
    for u in units:
        bias_diag = jnp.concatenate(
            [jnp.broadcast_to(bias_sc[u, pl.ds(blk0 + jj, 1), :], (SEL_BLOCK, tq))
             for jj in range(tq // SEL_BLOCK)], axis=0)
        for r in heads:
            s = jnp.where(diag_mask, s_sc[1, u, r] + bias_diag, NEG)
            m0 = jnp.max(s, axis=0, keepdims=True)
            m_sc[u, :, cols(r)] = m0
            acc_sc[u, :, cols(r)] = jnp.dot(vs_ref[0, u, qi], jnp.exp2(s - m0).astype(BF16),
                                            preferred_element_type=F32)

    def softmax_pv(kt, slot):
        for u in units:
            v_t = vs_ref[0, u, kt]
            for r in heads:
                s = s_sc[slot, u, r]
                m_old = m_sc[u, :, cols(r)]
                m_new = jnp.maximum(m_old, jnp.max(s, axis=0, keepdims=True))
                alpha = jnp.exp2(m_old - m_new)
                p = jnp.exp2(s - m_new)
                acc_sc[u, :, cols(r)] = alpha * acc_sc[u, :, cols(r)] + jnp.dot(
                    v_t, p.astype(BF16), preferred_element_type=F32)
                m_sc[u, :, cols(r)] = m_new


    def pair_body(j, carry):
        a = 2 * j
        qk_into(a + 1, 1)
        softmax_pv(a, 0)
        qk_into(a + 2, 0)
        softmax_pv(a + 1, 1)
        return carry

    lax.fori_loop(0, qi // 2, pair_body, 0)

    @pl.when(qi % 2 == 1)
    def _():
        softmax_pv(qi - 1, 0)

    for u in units:
        merged = []
        for r in heads:
            o_slc = acc_sc[u, 0:dk, cols(r)] * (1.0 / acc_sc[u, dk:dk + 1, cols(r)])
            merged.append(part_sc[u, :, cols(r)] + gate_row(u, r, 1) * o_slc)
        for pair in range(rep // 2):
            both = jnp.concatenate([merged[2 * pair], merged[2 * pair + 1]], axis=0)
            c0 = (u * (rep // 2) + pair) * LANES
            o_ref[:, c0:c0 + LANES] = both.T.astype(o_ref.dtype)


def _nsa_attention(q, gt, b_gate, kc, vc_t, ks, vs_t, kw, vw_t, msel_t, bsz, s):
    n = q.shape[0]
    tq = NSA_QTILE
    assert tq == WINDOW and s % tq == 0 and s // SEL_BLOCK == 32
    rep = NSA_HEADS // NSA_GROUPS
    dk = NSA_HEAD_DIM
    nq = s // tq
    ug = NSA_STEP_GROUPS
    per_bg = lambda a: pl.BlockSpec((1, ug) + a.shape[2:], lambda b, g, i: (b, g) + (0,) * (a.ndim - 2))
    return pl.pallas_call(
        _nsa_kernel,
        grid=(bsz, NSA_GROUPS // ug, nq),
        in_specs=[pl.BlockSpec(memory_space=pltpu.SMEM),
                  pl.BlockSpec((tq, ug * rep * LANES), lambda b, g, i: (b * nq + i, g)),
                  pl.BlockSpec((1, LANES, tq), lambda b, g, i: (b, 0, i)),
                  per_bg(kc), per_bg(vc_t), per_bg(ks), per_bg(vs_t), per_bg(kw), per_bg(vw_t),
                  pl.BlockSpec(msel_t.shape, lambda b, g, i: (0, 0))],
        out_specs=pl.BlockSpec((tq, ug * rep * dk), lambda b, g, i: (b * nq + i, g)),
        out_shape=jax.ShapeDtypeStruct((n, NSA_HEADS * dk), BF16),
        scratch_shapes=[pltpu.VMEM((ug, rep, tq, LANES), BF16),
                        pltpu.VMEM((2, ug, rep, tq, tq), F32),
                        pltpu.VMEM((ug, s // SEL_BLOCK, tq), F32),
                        pltpu.VMEM((ug, 1, rep * tq), F32),
                        pltpu.VMEM((ug, NSA_VROWS, rep * tq), F32),
                        pltpu.VMEM((ug, dk, rep * tq), F32)],
        compiler_params=_params(3),
        name="nsa_attention",
    )(b_gate, q, gt, kc, vc_t, ks, vs_t, kw, vw_t, msel_t)


def _sel_matrix_t(n_rows, n_sel):
    n_cmp = np.arange(n_rows)[None, :] * CMP_STRIDE
    sel = np.arange(n_sel)[:, None] * SEL_BLOCK
    ov = np.clip(np.minimum(n_cmp + CMP_BLOCK, sel + SEL_BLOCK) - np.maximum(n_cmp, sel), 0, None)
    out = np.zeros((LANES, n_rows), np.float32)
    out[LANES // 2:LANES // 2 + n_sel] = ov / CMP_BLOCK
    return jnp.asarray(out, dtype=BF16)


def _nsa_mixer(x, g, w_in, pe_k, k_w1, k_b1, k_w2, pe_v, v_w1, v_b1, v_w2, b_gate, w_out, bsz, s):
    n_gate = NSA_BRANCHES * NSA_HEADS
    kv = NSA_GROUPS * NSA_HEAD_DIM
    w_in_p = jnp.pad(w_in, ((0, 0), (0, LANES - n_gate))).astype(BF16)
    q, gt, kc_in, vc_in, ks, kw, vs_t, vw_t = _nsa_proj(x, g, w_in_p, bsz, s)
    kc = _compress(kc_in.reshape(bsz, s, kv), pe_k, k_w1, k_b1, k_w2, transpose_out=False)
    vc_t = _compress(vc_in.reshape(bsz, s, kv), pe_v, v_w1, v_b1, v_w2, transpose_out=True)
    msel_t = _sel_matrix_t(s // CMP_STRIDE, s // SEL_BLOCK)
    attn = _nsa_attention(q, gt, b_gate, kc, vc_t, ks, vs_t, kw, vw_t, msel_t, bsz, s)
    return [(attn, w_out.astype(BF16))]


def kernel(x, ffn1_norm, ffn1_w_gu, ffn1_w_down, mix_norm, ffn2_norm, ffn2_w_gu, ffn2_w_down,
           ab_w_in, rg_conv_w, rg_conv_b, rg_w_r, rg_b_r, rg_w_i, rg_b_i, rg_lambda,
           ml_conv_w, ml_conv_b, ml_b_i, ml_b_f, ml_norm, ab_w_out,
           nsa_w_in, nsa_pe_k, nsa_k_w1, nsa_k_b1, nsa_k_w2, nsa_pe_v, nsa_v_w1, nsa_v_b1, nsa_v_w2,
           nsa_b_gate, nsa_w_out, final_norm):
    bsz, s, d = x.shape
    depth = ffn1_norm.shape[0]
    h = x.reshape(bsz * s, d)
    w1_gu, w1_down = ffn1_w_gu.astype(BF16), ffn1_w_down.astype(BF16)
    w2_gu, w2_down = ffn2_w_gu.astype(BF16), ffn2_w_down.astype(BF16)
    for i in range(depth):
        j = i // 2
        h = _ffn(h, [], ffn1_norm[i], w1_gu, w1_down, final_norm, layer=i, final=False)
        if i % 2 == 0:
            mix = _ab_mixer(h, mix_norm[i], ab_w_in[j], rg_conv_w[j], rg_conv_b[j], rg_w_r[j], rg_b_r[j],
                            rg_w_i[j], rg_b_i[j], rg_lambda[j], ml_conv_w[j], ml_conv_b[j], ml_b_i[j],
                            ml_b_f[j], ml_norm[j], ab_w_out[j], bsz, s)
        else:
            mix = _nsa_mixer(h, mix_norm[i], nsa_w_in[j], nsa_pe_k[j], nsa_k_w1[j], nsa_k_b1[j],
                             nsa_k_w2[j], nsa_pe_v[j], nsa_v_w1[j], nsa_v_b1[j], nsa_v_w2[j],
                             nsa_b_gate[j], nsa_w_out[j], bsz, s)
        h = _ffn(h, mix, ffn2_norm[i], w2_gu, w2_down, final_norm, layer=i, final=(i == depth - 1))
    return h.reshape(bsz, s, d)
```

```python
import functools

import jax
import jax.numpy as jnp
import numpy as np
from jax import lax
from jax.experimental import pallas as pl
from jax.experimental.pallas import tpu as pltpu

F32 = jnp.float32
BF16 = jnp.bfloat16
HIGHEST = lax.Precision.HIGHEST

EPS = 1e-6
RG_C = 8.0
RG_BLOCKS = 8
CONV_WIDTH = 4
ML_HEADS = 4
ML_CHUNK = 128
NSA_HEADS = 16
NSA_GROUPS = 4
NSA_HEAD_DIM = 64
NSA_BRANCHES = 3
CMP_BLOCK = 32
CMP_STRIDE = 16
SEL_BLOCK = 64
SEL_TOPN = 8
WINDOW = 256
FORCED_SCORE = 1e6

LANES = 128
SUBLANES = 8
MXU_TILE = 256
NEG = -1e30
TAKEN = -2.0
VMEM_LIMIT = 48 * 1024 * 1024

TOKEN_TILE = 512
NSA_QTILE = 256
NSA_VROWS = 80
NSA_STEP_GROUPS = 4
LOG2E = 1.4426950408889634
NT_DIMS = (((1,), (1,)), ((), ()))


def _params(n_axes):
    return pltpu.CompilerParams(dimension_semantics=("arbitrary",) * n_axes,
                                vmem_limit_bytes=VMEM_LIMIT)


def _resident(shape, index_map):
    return pl.BlockSpec(shape, index_map, pipeline_mode=pl.Buffered(1))


def _rms(x, g):
    return x * lax.rsqrt(jnp.mean(x * x, axis=-1, keepdims=True) + EPS) * g


def _gelu_tanh(x):
    return 0.5 * x * (1.0 + jnp.tanh(0.7978845608028654 * (x + 0.044715 * (x * x * x))))


def _softplus(z):
    return jnp.maximum(z, 0.0) + jnp.log1p(jnp.exp(-jnp.abs(z)))


def _sigmoid(z):
    return 0.5 * jnp.tanh(0.5 * z) + 0.5


def _sqrt_nonneg(z):
    return jnp.where(z > 0.0, z * lax.rsqrt(z), 0.0)


def _ffn_kernel(*refs, n_mix, n_chunks, final):
    x_ref = refs[0]
    a_refs = refs[1:1 + n_mix]
    wo_refs = refs[1 + n_mix:1 + 2 * n_mix]
    g_ref, wg_ref, wu_ref, wd_ref, gf_ref, o_ref = refs[1 + 2 * n_mix:]
    x = x_ref[...]
    for a_ref, wo_ref in zip(a_refs, wo_refs):
        x = x + jnp.dot(a_ref[...], wo_ref[...], preferred_element_type=F32)
    xn = _rms(x, g_ref[...]).astype(BF16)
    n_tiles = wg_ref.shape[1] // MXU_TILE
    bounds = [MXU_TILE * ((n_tiles * c + n_chunks - 1) // n_chunks) for c in range(n_chunks + 1)]
    acc = None
    for lo, hi in zip(bounds[:-1], bounds[1:]):
        gate = jnp.dot(xn, wg_ref[:, lo:hi], preferred_element_type=F32)
        up = jnp.dot(xn, wu_ref[:, lo:hi], preferred_element_type=F32)
        h = (gate * jax.nn.sigmoid(gate) * up).astype(BF16)
        part = jnp.dot(h, wd_ref[lo:hi, :], preferred_element_type=F32)
        acc = part if acc is None else acc + part
    y = x + 0.5 * acc
    if final:
        y = _rms(y, gf_ref[...])
    o_ref[...] = y


def _ffn(x, mix, g, w_gu, w_down, gf, *, layer, final):
    n, d = x.shape
    f = w_down.shape[1]
    tm = min(TOKEN_TILE, n)
    const = lambda i: (0, 0)
    row = lambda i: (i, 0)
    acts = [a for a, _ in mix]
    wos = [w for _, w in mix]
    return pl.pallas_call(
        functools.partial(_ffn_kernel, n_mix=len(mix), n_chunks=2, final=final),
        grid=(n // tm,),
        in_specs=[pl.BlockSpec((tm, d), row)]
        + [pl.BlockSpec((tm, a.shape[1]), row) for a in acts]
        + [_resident(w.shape, const) for w in wos]
        + [pl.BlockSpec((1, d), const),
           _resident((None, d, f), lambda i: (layer, 0, 0)),
           _resident((None, d, f), lambda i: (layer, 0, 1)),
           _resident((None, f, d), lambda i: (layer, 0, 0)),
           pl.BlockSpec((1, d), const)],
        out_specs=pl.BlockSpec((tm, d), row),
        out_shape=jax.ShapeDtypeStruct((n, d), F32),
        compiler_params=_params(1),
        name="ffn_final" if final else ("ffn_mix" if mix else "ffn"),
    )(x, *acts, *wos, g.reshape(1, d), w_gu, w_gu, w_down, gf.reshape(1, d))


def _norm_matmul_kernel(x_ref, g_ref, w_ref, o_ref):
    xn = _rms(x_ref[...], g_ref[...]).astype(BF16)
    o_ref[...] = jnp.dot(xn, w_ref[...], preferred_element_type=F32)


def _norm_matmul(x, g, w):
    n, d = x.shape
    dout = w.shape[1]
    tm = min(TOKEN_TILE, n)
    const = lambda i: (0, 0)
    return pl.pallas_call(
        _norm_matmul_kernel,
        grid=(n // tm,),
        in_specs=[pl.BlockSpec((tm, d), lambda i: (i, 0)),
                  pl.BlockSpec((1, d), const),
                  _resident((d, dout), const)],
        out_specs=pl.BlockSpec((tm, dout), lambda i: (i, 0)),
        out_shape=jax.ShapeDtypeStruct((n, dout), F32),
        compiler_params=_params(1),
        name="norm_matmul",
    )(x, g.reshape(1, d), w)


def _causal_conv(x, cw, cb, pad_ref):
    s = x.shape[0]
    pad_ref[0:SUBLANES, :] = jnp.zeros((SUBLANES, x.shape[1]), F32)
    pad_ref[SUBLANES:SUBLANES + s, :] = x
    y = cb + cw[CONV_WIDTH - 1:CONV_WIDTH, :] * x
    for j in range(CONV_WIDTH - 1):
        off = SUBLANES - (CONV_WIDTH - 1 - j)
        y = y + cw[j:j + 1, :] * pad_ref[off:off + s, :]
    return y


def _causal_conv_blocks(x, cw, cb, pad_ref, emit, block=256):
    s = x.shape[0]
    pad_ref[0:SUBLANES, :] = jnp.zeros((SUBLANES, x.shape[1]), F32)
    pad_ref[SUBLANES:SUBLANES + s, :] = x
    for r0 in range(0, s, block):
        y = cb + cw[CONV_WIDTH - 1:CONV_WIDTH, :] * pad_ref[SUBLANES + r0:SUBLANES + r0 + block, :]
        for j in range(CONV_WIDTH - 1):
            off = r0 + SUBLANES - (CONV_WIDTH - 1 - j)
            y = y + cw[j:j + 1, :] * pad_ref[off:off + block, :]
        emit(r0, y)


def _rglru_kernel(xa_ref, ga_ref, cw_ref, cb_ref, wr_ref, br_ref, wi_ref, bi_ref, lam_ref,
                  o_ref, pad_ref, a_ref, u_ref):
    s = xa_ref.shape[1]
    ng = s // SUBLANES
    xc = _causal_conv(xa_ref[0], cw_ref[...], cb_ref[...], pad_ref)
    xb = xc.astype(BF16)
    r = _sigmoid(jnp.dot(xb, wr_ref[...], preferred_element_type=F32) + br_ref[...])
    ig = _sigmoid(jnp.dot(xb, wi_ref[...], preferred_element_type=F32) + bi_ref[...])
    log_a = -RG_C * r * _softplus(-lam_ref[...])
    a = jnp.exp(log_a)
    th = jnp.tanh(log_a)
    u = _sqrt_nonneg(-2.0 * th / (1.0 - th)) * (ig * xc)
    a3 = a.reshape(ng, SUBLANES, LANES)
    u3 = u.reshape(ng, SUBLANES, LANES)
    row = lax.broadcasted_iota(jnp.int32, (ng, SUBLANES, LANES), 1)
    sh = 1
    while sh < SUBLANES:
        a_s = pltpu.roll(a3, sh, 1)
        u_s = pltpu.roll(u3, sh, 1)
        m = row >= sh
        u3 = jnp.where(m, a3 * u_s + u3, u3)
        a3 = jnp.where(m, a3 * a_s, a3)
        sh *= 2
    a_ref[...] = a3
    u_ref[...] = u3

    def body(i, h):
        hh = a_ref[i] * h + u_ref[i]
        u_ref[i] = hh
        return hh[SUBLANES - 1:SUBLANES, :]

    lax.fori_loop(0, ng, body, jnp.zeros((1, LANES), F32), unroll=True)
    o_ref[0] = (_gelu_tanh(ga_ref[0]) * u_ref[...].reshape(s, LANES)).astype(o_ref.dtype)


def _rglru(proj3, cw, cb, wr_bd, br, wi_bd, bi, lam):
    bsz, s, _ = proj3.shape
    c = cw.shape[1]
    nb = c // LANES
    vec = lambda b, j: (0, j)
    return pl.pallas_call(
        _rglru_kernel,
        grid=(bsz, nb),
        in_specs=[pl.BlockSpec((1, s, LANES), lambda b, j: (b, 0, j)),
                  pl.BlockSpec((1, s, LANES), lambda b, j: (b, 0, nb + j)),
                  pl.BlockSpec((CONV_WIDTH, LANES), vec),
                  pl.BlockSpec((1, LANES), vec),
                  pl.BlockSpec((LANES, LANES), lambda b, j: (j, j)),
                  pl.BlockSpec((1, LANES), vec),
                  pl.BlockSpec((LANES, LANES), lambda b, j: (j, j)),
                  pl.BlockSpec((1, LANES), vec),
                  pl.BlockSpec((1, LANES), vec)],
        out_specs=pl.BlockSpec((1, s, LANES), lambda b, j: (b, 0, j)),
        out_shape=jax.ShapeDtypeStruct((bsz, s, c), BF16),
        scratch_shapes=[pltpu.VMEM((s + SUBLANES, LANES), F32),
                        pltpu.VMEM((s // SUBLANES, SUBLANES, LANES), F32),
                        pltpu.VMEM((s // SUBLANES, SUBLANES, LANES), F32)],
        compiler_params=_params(2),
        name="rglru",
    )(proj3, proj3, cw, cb.reshape(1, c), wr_bd, br.reshape(1, c), wi_bd, bi.reshape(1, c),
      lam.reshape(1, c))


def _mlstm_kernel(bias_ref, q_ref, k_ref, v_ref, og_ref, gi_ref, gf_ref, cwq_ref, cbq_ref,
                  cwk_ref, cbk_ref, ng_ref, o_ref, pad_ref, q_sc, k_sc, li_sc, lf_sc, bc_sc):
    hd = pl.program_id(1)
    s = q_ref.shape[1]
    dh = q_ref.shape[2]
    nc = s // ML_CHUNK

    def put_q(r0, y):
        q_sc[r0:r0 + y.shape[0], :] = jax.nn.silu(y) * (dh ** -0.5)

    def put_k(r0, y):
        k_sc[r0:r0 + y.shape[0], :] = jax.nn.silu(y)

    _causal_conv_blocks(q_ref[0], cwq_ref[...], cbq_ref[...], pad_ref, put_q)
    _causal_conv_blocks(k_ref[0], cwk_ref[...], cbk_ref[...], pad_ref, put_k)
    li_sc[...] = gi_ref[0, 0] + bias_ref[hd]
    lf = -_softplus(-(gf_ref[0, 0] + bias_ref[ML_HEADS + hd]))
    lf_sc[...] = lf
    jj = lax.broadcasted_iota(jnp.int32, (ML_CHUNK, ML_CHUNK), 0)
    kk = lax.broadcasted_iota(jnp.int32, (ML_CHUNK, ML_CHUNK), 1)
    tri = kk <= jj
    upper = jnp.where(jj <= kk, 1.0, 0.0).astype(F32)
    bc_sc[...] = jnp.dot(lf, upper, precision=HIGHEST, preferred_element_type=F32)
    norm_g = ng_ref[...]

    def chunk(c, carry):
        c_st, n_st, m_st = carry
        r0 = pl.multiple_of(c * ML_CHUNK, ML_CHUNK)
        qc = q_sc[pl.ds(r0, ML_CHUNK), :]
        kc = k_sc[pl.ds(r0, ML_CHUNK), :]
        vc = v_ref[0, pl.ds(r0, ML_CHUNK), :]
        li_row = li_sc[pl.ds(c, 1), :]
        lf_row = lf_sc[pl.ds(c, 1), :]
        b_row = bc_sc[pl.ds(c, 1), :]
        b_col = jnp.sum(jnp.where(tri, lf_row, 0.0), axis=1, keepdims=True)
        g_tot = jnp.sum(lf_row, axis=1, keepdims=True)
        w_row = g_tot - b_row + li_row
        m_loc = jnp.max(w_row, axis=1, keepdims=True)
        wk = jnp.exp(w_row - m_loc)
        lhs = jnp.concatenate([vc.T * wk, jnp.broadcast_to(wk, (2 * SUBLANES, ML_CHUNK))], axis=0)
        cn = jnp.dot(lhs.astype(BF16), kc.astype(BF16), preferred_element_type=F32)
        c_loc = cn[0:dh]
        n_loc = cn[dh:dh + 1]
        d = jnp.where(tri, b_col - b_row + li_row, NEG)
        m_inter = b_col + m_st
        m = jnp.maximum(m_inter, jnp.max(d, axis=1, keepdims=True))
        qb = qc.astype(BF16)
        qk = lax.dot_general(qb, kc.astype(BF16), NT_DIMS, preferred_element_type=F32)
        p = jnp.exp(d - m) * qk
        sc = jnp.exp(m_inter - m)
        inter = lax.dot_general(qb, c_st.astype(BF16), NT_DIMS, preferred_element_type=F32)
        num = sc * inter + jnp.dot(p.astype(BF16), vc.astype(BF16), preferred_element_type=F32)
        den = jnp.sum(sc * (qc * n_st) + p, axis=1, keepdims=True)
        h = num / jnp.maximum(jnp.abs(den), jnp.exp(-m))
        h = h * lax.rsqrt(jnp.mean(h * h, axis=1, keepdims=True) + EPS) * norm_g
        o_ref[0, pl.ds(r0, ML_CHUNK), :] = (
            jax.nn.sigmoid(og_ref[0, pl.ds(r0, ML_CHUNK), :]) * h).astype(o_ref.dtype)
        m_new = jnp.maximum(g_tot + m_st, m_loc)
        sa = jnp.exp(g_tot + m_st - m_new)
        sb = jnp.exp(m_loc - m_new)
        return sa * c_st + sb * c_loc, sa * n_st + sb * n_loc, m_new

    init = (jnp.zeros((dh, dh), F32), jnp.zeros((1, dh), F32), jnp.full((1, 1), NEG, F32))
    lax.fori_loop(0, nc, chunk, init, unroll=True)


def _mlstm(proj3, gates_t, bias, cw, cb, norm_g, *, col0):
    bsz, s, _ = proj3.shape
    nh = ML_HEADS
    dh = LANES
    nc = s // ML_CHUNK
    base = col0 // dh
    blk = lambda off: pl.BlockSpec((1, s, dh), lambda b, h: (b, 0, base + off + h))
    vec = lambda off: (lambda b, h: (0, off + h))
    return pl.pallas_call(
        _mlstm_kernel,
        grid=(bsz, nh),
        in_specs=[pl.BlockSpec(memory_space=pltpu.SMEM),
                  blk(0), blk(nh), blk(2 * nh), blk(3 * nh),
                  pl.BlockSpec((1, 1, nc, ML_CHUNK), lambda b, h: (b, h, 0, 0)),
                  pl.BlockSpec((1, 1, nc, ML_CHUNK), lambda b, h: (b, nh + h, 0, 0)),
                  pl.BlockSpec((CONV_WIDTH, dh), vec(0)), pl.BlockSpec((1, dh), vec(0)),
                  pl.BlockSpec((CONV_WIDTH, dh), vec(nh)), pl.BlockSpec((1, dh), vec(nh)),
                  pl.BlockSpec((1, dh), vec(0))],
        out_specs=pl.BlockSpec((1, s, dh), lambda b, h: (b, 0, h)),
        out_shape=jax.ShapeDtypeStruct((bsz, s, nh * dh), BF16),
        scratch_shapes=[pltpu.VMEM((s + SUBLANES, dh), F32),
                        pltpu.VMEM((s, dh), F32), pltpu.VMEM((s, dh), F32),
                        pltpu.VMEM((nc, ML_CHUNK), F32), pltpu.VMEM((nc, ML_CHUNK), F32),
                        pltpu.VMEM((nc, ML_CHUNK), F32)],
        compiler_params=_params(2),
        name="mlstm",
    )(bias, proj3, proj3, proj3, proj3, gates_t, gates_t, cw, cb.reshape(1, -1), cw,
      cb.reshape(1, -1), norm_g.reshape(1, -1))


def _ab_mixer(x, g, w_in, rg_conv_w, rg_conv_b, rg_w_r, rg_b_r, rg_w_i, rg_b_i, rg_lambda,
              ml_conv_w, ml_conv_b, ml_b_i, ml_b_f, ml_norm, w_out, bsz, s):
    d_rg = rg_conv_w.shape[1]
    d_ml = ml_norm.shape[0]
    d_main = 2 * d_rg + 4 * d_ml
    n_gate = 2 * ML_HEADS
    w_in_p = jnp.pad(w_in, ((0, 0), (0, LANES - n_gate))).astype(BF16)
    proj = _norm_matmul(x, g, w_in_p)
    proj3 = proj.reshape(bsz, s, proj.shape[1])
    bd = lambda w: jax.scipy.linalg.block_diag(*[w[i] for i in range(RG_BLOCKS)]).astype(BF16)
    ya = _rglru(proj3, rg_conv_w, rg_conv_b, bd(rg_w_r), rg_b_r, bd(rg_w_i), rg_b_i, rg_lambda)
    gates_t = proj3[:, :, d_main:d_main + n_gate].transpose(0, 2, 1).reshape(
        bsz, n_gate, s // ML_CHUNK, ML_CHUNK)
    yb = _mlstm(proj3, gates_t, jnp.concatenate([ml_b_i, ml_b_f]), ml_conv_w, ml_conv_b, ml_norm,
                col0=2 * d_rg)
    w_out_b = w_out.astype(BF16)
    return [(ya.reshape(bsz * s, d_rg), w_out_b[:d_rg]), (yb.reshape(bsz * s, d_ml), w_out_b[d_rg:])]


def _nsa_proj_kernel(x_ref, g_ref, w_ref, q_ref, gt_ref, kc_ref, vc_ref, ks_ref, kw_ref, vs_ref, vw_ref,
                     *, seq):
    tm = x_ref.shape[0]
    kv = NSA_GROUPS * NSA_HEAD_DIM
    width = NSA_HEADS * NSA_HEAD_DIM
    half = LANES // 2
    xn = _rms(x_ref[...], g_ref[...]).astype(BF16)
    proj = jnp.dot(xn, w_ref[...], preferred_element_type=F32)
    lane = lax.broadcasted_iota(jnp.int32, (tm, LANES), 1)
    lo = lane < half

    def lane_pair(c0):
        p = proj[:, c0:c0 + LANES]
        return p, pltpu.roll(p, half, 1)

    scale = NSA_HEAD_DIM ** -0.5 * LOG2E
    for pair in range(NSA_HEADS // 2):
        for j, piece in enumerate(lane_pair(pair * LANES)):
            h = 2 * pair + j
            q_ref[:, h * LANES:(h + 1) * LANES] = jnp.where(lo, piece * scale, 0.0).astype(BF16)
    gt_ref[0] = proj[:, width + 6 * kv:width + 6 * kv + LANES].T
    kc_ref[...] = proj[:, width:width + kv]
    vc_ref[...] = proj[:, width + kv:width + 2 * kv]
    pos = (pl.program_id(0) % (seq // tm)) * tm + lax.broadcasted_iota(jnp.int32, (tm, LANES), 0)
    onehot = jnp.where(lane == half + pos // SEL_BLOCK, 1.0, 0.0)
    for pair in range(NSA_GROUPS // 2):
        for idx, ref, fill in ((2, ks_ref, onehot), (4, kw_ref, 0.0)):
            for j, piece in enumerate(lane_pair(width + idx * kv + pair * LANES)):
                ref[0, 2 * pair + j] = jnp.where(lo, piece, fill).astype(BF16)
        for idx, ref in ((3, vs_ref), (5, vw_ref)):
            c0 = width + idx * kv + pair * LANES
            t = proj[:, c0:c0 + LANES].T.astype(BF16)
            ones = jnp.ones((NSA_VROWS - half, NSA_QTILE), BF16)
            for j in range(tm // NSA_QTILE):
                for hf in range(2):
                    ref[0, 2 * pair + hf, j, 0:half, :] = t[hf * half:(hf + 1) * half,
                                                            j * NSA_QTILE:(j + 1) * NSA_QTILE]
                    ref[0, 2 * pair + hf, j, half:NSA_VROWS, :] = ones


def _nsa_proj(x, g, w, bsz, s):
    n, d = x.shape
    dout = w.shape[1]
    tm = min(TOKEN_TILE, s)
    nsb = s // tm
    ng, dk = NSA_GROUPS, NSA_HEAD_DIM
    kv = ng * dk
    tq = NSA_QTILE
    const = lambda i: (0, 0)
    row = lambda i: (i, 0)
    keyed = lambda i: (i // nsb, 0, i % nsb, 0)
    return pl.pallas_call(
        functools.partial(_nsa_proj_kernel, seq=s),
        grid=(n // tm,),
        in_specs=[pl.BlockSpec((tm, d), row), pl.BlockSpec((1, d), const), _resident((d, dout), const)],
        out_specs=[pl.BlockSpec((tm, NSA_HEADS * LANES), row),
                   pl.BlockSpec((1, LANES, tm), lambda i: (i // nsb, 0, i % nsb)),
                   pl.BlockSpec((tm, kv), row), pl.BlockSpec((tm, kv), row),
                   pl.BlockSpec((1, ng, tm, LANES), keyed), pl.BlockSpec((1, ng, tm, LANES), keyed),
                   pl.BlockSpec((1, ng, tm // tq, NSA_VROWS, tq), lambda i: (i // nsb, 0, i % nsb, 0, 0)),
                   pl.BlockSpec((1, ng, tm // tq, NSA_VROWS, tq), lambda i: (i // nsb, 0, i % nsb, 0, 0))],
        out_shape=[jax.ShapeDtypeStruct((n, NSA_HEADS * LANES), BF16),
                   jax.ShapeDtypeStruct((bsz, LANES, s), F32),
                   jax.ShapeDtypeStruct((n, kv), F32), jax.ShapeDtypeStruct((n, kv), F32),
                   jax.ShapeDtypeStruct((bsz, ng, s, LANES), BF16),
                   jax.ShapeDtypeStruct((bsz, ng, s, LANES), BF16),
                   jax.ShapeDtypeStruct((bsz, ng, s // tq, NSA_VROWS, tq), BF16),
                   jax.ShapeDtypeStruct((bsz, ng, s // tq, NSA_VROWS, tq), BF16)],
        compiler_params=_params(1),
        name="nsa_proj",
    )(x, g.reshape(1, d), w)


def _compress_kernel(x_ref, pe_ref, w1_ref, b1_ref, w2_ref, o_ref, *, transpose_out):
    nb = x_ref.shape[1] // CMP_STRIDE
    top = bot = None
    for l in range(CMP_STRIDE):
        rows = x_ref[0, pl.ds(l, nb, stride=CMP_STRIDE), :]
        t = jnp.dot((rows + pe_ref[l:l + 1, :]).astype(BF16), w1_ref[l], preferred_element_type=F32)
        b = jnp.dot((rows + pe_ref[CMP_STRIDE + l:CMP_STRIDE + l + 1, :]).astype(BF16),
                    w1_ref[CMP_STRIDE + l], preferred_element_type=F32)
        top = t if top is None else top + t
        bot = b if bot is None else bot + b
    pre = top + pltpu.roll(bot, nb - 1, 0) + b1_ref[...]
    out = jnp.dot(_gelu_tanh(pre).astype(BF16), w2_ref[...], preferred_element_type=F32)
    out = jnp.where(lax.broadcasted_iota(jnp.int32, out.shape, 0) < nb - 1, out, 0.0)
    for hf in range(2):
        piece = out[:, hf * LANES:(hf + 1) * LANES]
        if transpose_out:
            o_ref[0, hf] = piece.T[0:NSA_HEAD_DIM, :].astype(o_ref.dtype)
        else:
            o_ref[0, hf] = piece.astype(o_ref.dtype)


def _compress(kv_in, pe, w1, b1, w2, *, transpose_out):
    bsz, s, _ = kv_in.shape
    dk, hid = NSA_HEAD_DIM, w1.shape[1]
    nb = s // CMP_STRIDE
    w1r = w1.reshape(CMP_BLOCK, dk, hid)
    z = jnp.zeros_like(w1r)
    w1p = jnp.concatenate([jnp.concatenate([w1r, z], axis=2), jnp.concatenate([z, w1r], axis=2)],
                          axis=1).astype(BF16)
    w2w = jnp.pad(w2, ((0, 0), (0, LANES - dk)))
    z2 = jnp.zeros_like(w2w)
    w2p = jnp.concatenate([jnp.concatenate([w2w, z2], axis=1), jnp.concatenate([z2, w2w], axis=1)],
                          axis=0).astype(BF16)
    const2 = lambda b, p: (0, 0)
    out_block = (1, 2, dk, nb) if transpose_out else (1, 2, nb, LANES)
    out_shape = (bsz, NSA_GROUPS, dk, nb) if transpose_out else (bsz, NSA_GROUPS, nb, LANES)
    return pl.pallas_call(
        functools.partial(_compress_kernel, transpose_out=transpose_out),
        grid=(bsz, NSA_GROUPS // 2),
        in_specs=[pl.BlockSpec((1, s, LANES), lambda b, p: (b, 0, p)),
                  pl.BlockSpec((CMP_BLOCK, LANES), const2),
                  _resident((CMP_BLOCK, LANES, 2 * hid), lambda b, p: (0, 0, 0)),
                  pl.BlockSpec((1, 2 * hid), const2),
                  _resident((2 * hid, 2 * LANES), const2)],
        out_specs=pl.BlockSpec(out_block, lambda b, p: (b, p, 0, 0)),
        out_shape=jax.ShapeDtypeStruct(out_shape, BF16),
        compiler_params=_params(2),
        name="nsa_compress",
    )(kv_in, jnp.concatenate([pe, pe], axis=1), w1p, jnp.concatenate([b1, b1]).reshape(1, 2 * hid), w2p)


def _nsa_kernel(*refs):
    step = pl.program_id(2)
    q_ref, ks_ref = refs[1], refs[5]
    for qi in range(ks_ref.shape[2] // q_ref.shape[0]):
        pl.when(step == qi)(functools.partial(_nsa_step, qi, *refs))


def _nsa_step(qi, bg_ref, q_ref, gt_ref, kc_ref, vc_ref, ks_ref, vs_ref, kw_ref, vw_ref, msel_ref,
              o_ref, qa_sc, s_sc, bias_sc, m_sc, acc_sc, part_sc):
    gp = pl.program_id(1)
    tq = q_ref.shape[0]
    rep = NSA_HEADS // NSA_GROUPS
    dk = NSA_HEAD_DIM
    half = LANES // 2
    n_sel = 32
    units = range(NSA_STEP_GROUPS)
    heads = range(rep)
    s0 = qi * tq
    key_off = lax.broadcasted_iota(jnp.int32, (tq, tq), 0)
    qry_off = lax.broadcasted_iota(jnp.int32, (tq, tq), 1)
    cols = lambda r: slice(r * tq, (r + 1) * tq)
    nt_dot = lambda k, q: lax.dot_general(k, q, NT_DIMS, preferred_element_type=F32)
    q_head = lambda u, r: q_ref[:, (u * rep + r) * LANES:(u * rep + r + 1) * LANES]

    def gate_row(u, r, br):
        idx = (gp * NSA_STEP_GROUPS + u) * (rep * NSA_BRANCHES) + r * NSA_BRANCHES + br
        return jax.nn.sigmoid(gt_ref[0, pl.ds(idx, 1), :] + bg_ref[idx])

    has_prev = qi >= 1
    kp = max(qi - 1, 0)
    p0 = kp * tq
    d0 = qi * tq
    cmp_scores = [[nt_dot(kc_ref[0, u], q_head(u, r)) for r in heads] for u in units]
    win_scores = [[(nt_dot(kw_ref[0, u, pl.ds(p0, tq), :], q_head(u, r)) if has_prev else None,
                    nt_dot(kw_ref[0, u, pl.ds(d0, tq), :], q_head(u, r))) for r in heads] for u in units]
    for u in units:
        for r in heads:
            s_sc[1, u, r] = nt_dot(ks_ref[0, u, pl.ds(d0, tq), :], q_head(u, r))

    n_blk = kc_ref.shape[2]
    blk_id = lax.broadcasted_iota(jnp.int32, (n_blk, tq), 0)
    t_blk = s0 + lax.broadcasted_iota(jnp.int32, (n_blk, tq), 1)
    cmask = (blk_id * CMP_STRIDE + (CMP_BLOCK - 1) <= t_blk) & (blk_id < n_blk - 1)
    psum = [None] * NSA_STEP_GROUPS
    for r in heads:
        for u in units:
            scm = jnp.where(cmask, cmp_scores[u][r], NEG)
            e = jnp.where(cmask, jnp.exp2(scm - jnp.max(scm, axis=0, keepdims=True)), 0.0)
            den = jnp.sum(e, axis=0, keepdims=True)
            p = e * (1.0 / jnp.where(den > 0.0, den, 1.0))
            psum[u] = p if psum[u] is None else psum[u] + p
            part_sc[u, :, cols(r)] = gate_row(u, r, 0) * jnp.dot(vc_ref[0, u], p.astype(BF16),
                                                                 preferred_element_type=F32)

    jb = lax.broadcasted_iota(jnp.int32, (n_sel, tq), 0)
    jbf = jb.astype(F32)
    cur = (s0 + lax.broadcasted_iota(jnp.int32, (n_sel, tq), 1)) // SEL_BLOCK
    valid = jb <= cur
    forced = ((jb == 0) | (jb == cur) | (jb == cur - 1)) & valid
    for u in units:
        p_hi = psum[u].astype(BF16)
        p_lo = (psum[u] - p_hi.astype(F32)).astype(BF16)
        score_t = (jnp.dot(msel_ref[...], p_hi, preferred_element_type=F32)
                   + jnp.dot(msel_ref[...], p_lo, preferred_element_type=F32))
        s2 = jnp.where(forced, FORCED_SCORE, jnp.where(valid, score_t[half:half + n_sel, :], -1.0))
        for _ in range(SEL_TOPN):
            best = jnp.max(s2, axis=0, keepdims=True)
            first = jnp.min(jnp.where(s2 == best, jbf, float(n_sel)), axis=0, keepdims=True)
            s2 = jnp.where(jbf == first, TAKEN, s2)
        bias_t = jnp.where((s2 == TAKEN) & valid, 0.0, NEG)
        bias_sc[u] = bias_t
        bias_full = jnp.concatenate(
            [jnp.zeros((half, tq), F32), bias_t, jnp.zeros((LANES - half - n_sel, tq), F32)], axis=0)
        bias_q = bias_full.T
        for r in heads:
            qa_sc[u, r] = (q_head(u, r).astype(F32) + bias_q).astype(BF16)

    prev_mask = key_off > qry_off
    diag_mask = key_off <= qry_off
    for r in heads:
        for u in units:
            s_diag = jnp.where(diag_mask, win_scores[u][r][1], NEG)
            mw = jnp.max(s_diag, axis=0, keepdims=True)
            if has_prev:
                s_prev = jnp.where(prev_mask, win_scores[u][r][0], NEG)
                mw = jnp.maximum(jnp.max(s_prev, axis=0, keepdims=True), mw)
            win = jnp.dot(vw_ref[0, u, qi], jnp.exp2(s_diag - mw).astype(BF16),
                          preferred_element_type=F32)
            if has_prev:
                win = win + jnp.dot(vw_ref[0, u, kp], jnp.exp2(s_prev - mw).astype(BF16),
                                    preferred_element_type=F32)
            part_sc[u, :, cols(r)] += gate_row(u, r, 2) * (win[0:dk] * (1.0 / win[dk:dk + 1]))

    def qk_into(kt, slot):
        for u in units:
            k_tile = ks_ref[0, u, kt * tq:(kt + 1) * tq, :]
            for r in heads:
                s_sc[slot, u, r] = nt_dot(k_tile, qa_sc[u, r])

    if qi >= 1:
        qk_into(0, 0)
    blk0 = qi * (tq // SEL_BLOCK)
    for u in units:
        bias_diag = jnp.concatenate(
            [jnp.broadcast_to(bias_sc[u, pl.ds(blk0 + jj, 1), :], (SEL_BLOCK, tq))
             for jj in range(tq // SEL_BLOCK)], axis=0)
        for r in heads:
            s = jnp.where(diag_mask, s_sc[1, u, r] + bias_diag, NEG)
            m0 = jnp.max(s, axis=0, keepdims=True)
            m_sc[u, :, cols(r)] = m0
            acc_sc[u, :, cols(r)] = jnp.dot(vs_ref[0, u, qi], jnp.exp2(s - m0).astype(BF16),
                                            preferred_element_type=F32)

    def softmax_pv(kt, slot):
        for u in units:
            v_t = vs_ref[0, u, kt]
            for r in heads:
                s = s_sc[slot, u, r]
                m_old = m_sc[u, :, cols(r)]
                m_new = jnp.maximum(m_old, jnp.max(s, axis=0, keepdims=True))
                alpha = jnp.exp2(m_old - m_new)
                p = jnp.exp2(s - m_new)
                acc_sc[u, :, cols(r)] = alpha * acc_sc[u, :, cols(r)] + jnp.dot(
                    v_t, p.astype(BF16), preferred_element_type=F32)
                m_sc[u, :, cols(r)] = m_new

    for kt in range(qi):
        if kt + 1 < qi:
            qk_into(kt + 1, (kt + 1) % 2)
        softmax_pv(kt, kt % 2)

    for u in units:
        merged = []
        for r in heads:
            o_slc = acc_sc[u, 0:dk, cols(r)] * (1.0 / acc_sc[u, dk:dk + 1, cols(r)])
            merged.append(part_sc[u, :, cols(r)] + gate_row(u, r, 1) * o_slc)
        for pair in range(rep // 2):
            both = jnp.concatenate([merged[2 * pair], merged[2 * pair + 1]], axis=0)
            c0 = (u * (rep // 2) + pair) * LANES
            o_ref[:, c0:c0 + LANES] = both.T.astype(o_ref.dtype)


def _nsa_attention(q, gt, b_gate, kc, vc_t, ks, vs_t, kw, vw_t, msel_t, bsz, s):
    n = q.shape[0]
    tq = NSA_QTILE
    assert tq == WINDOW and s % tq == 0 and s // SEL_BLOCK == 32
    rep = NSA_HEADS // NSA_GROUPS
    dk = NSA_HEAD_DIM
    nq = s // tq
    ug = NSA_STEP_GROUPS
    per_bg = lambda a: pl.BlockSpec((1, ug) + a.shape[2:], lambda b, g, i: (b, g) + (0,) * (a.ndim - 2))
    return pl.pallas_call(
        _nsa_kernel,
        grid=(bsz, NSA_GROUPS // ug, nq),
        in_specs=[pl.BlockSpec(memory_space=pltpu.SMEM),
                  pl.BlockSpec((tq, ug * rep * LANES), lambda b, g, i: (b * nq + i, g)),
                  pl.BlockSpec((1, LANES, tq), lambda b, g, i: (b, 0, i)),
                  per_bg(kc), per_bg(vc_t), per_bg(ks), per_bg(vs_t), per_bg(kw), per_bg(vw_t),
                  pl.BlockSpec(msel_t.shape, lambda b, g, i: (0, 0))],
        out_specs=pl.BlockSpec((tq, ug * rep * dk), lambda b, g, i: (b * nq + i, g)),
        out_shape=jax.ShapeDtypeStruct((n, NSA_HEADS * dk), BF16),
        scratch_shapes=[pltpu.VMEM((ug, rep, tq, LANES), BF16),
                        pltpu.VMEM((2, ug, rep, tq, tq), F32),
                        pltpu.VMEM((ug, s // SEL_BLOCK, tq), F32),
                        pltpu.VMEM((ug, 1, rep * tq), F32),
                        pltpu.VMEM((ug, NSA_VROWS, rep * tq), F32),
                        pltpu.VMEM((ug, dk, rep * tq), F32)],
        compiler_params=_params(3),
        name="nsa_attention",
    )(b_gate, q, gt, kc, vc_t, ks, vs_t, kw, vw_t, msel_t)


def _sel_matrix_t(n_rows, n_sel):
    n_cmp = np.arange(n_rows)[None, :] * CMP_STRIDE
    sel = np.arange(n_sel)[:, None] * SEL_BLOCK
    ov = np.clip(np.minimum(n_cmp + CMP_BLOCK, sel + SEL_BLOCK) - np.maximum(n_cmp, sel), 0, None)
    out = np.zeros((LANES, n_rows), np.float32)
    out[LANES // 2:LANES // 2 + n_sel] = ov / CMP_BLOCK
    return jnp.asarray(out, dtype=BF16)


def _nsa_mixer(x, g, w_in, pe_k, k_w1, k_b1, k_w2, pe_v, v_w1, v_b1, v_w2, b_gate, w_out, bsz, s):
    n_gate = NSA_BRANCHES * NSA_HEADS
    kv = NSA_GROUPS * NSA_HEAD_DIM
    w_in_p = jnp.pad(w_in, ((0, 0), (0, LANES - n_gate))).astype(BF16)
    q, gt, kc_in, vc_in, ks, kw, vs_t, vw_t = _nsa_proj(x, g, w_in_p, bsz, s)
    kc = _compress(kc_in.reshape(bsz, s, kv), pe_k, k_w1, k_b1, k_w2, transpose_out=False)
    vc_t = _compress(vc_in.reshape(bsz, s, kv), pe_v, v_w1, v_b1, v_w2, transpose_out=True)
    msel_t = _sel_matrix_t(s // CMP_STRIDE, s // SEL_BLOCK)
    attn = _nsa_attention(q, gt, b_gate, kc, vc_t, ks, vs_t, kw, vw_t, msel_t, bsz, s)
    return [(attn, w_out.astype(BF16))]


def kernel(x, ffn1_norm, ffn1_w_gu, ffn1_w_down, mix_norm, ffn2_norm, ffn2_w_gu, ffn2_w_down,
           ab_w_in, rg_conv_w, rg_conv_b, rg_w_r, rg_b_r, rg_w_i, rg_b_i, rg_lambda,
           ml_conv_w, ml_conv_b, ml_b_i, ml_b_f, ml_norm, ab_w_out,
           nsa_w_in, nsa_pe_k, nsa_k_w1, nsa_k_b1, nsa_k_w2, nsa_pe_v, nsa_v_w1, nsa_v_b1, nsa_v_w2,
           nsa_b_gate, nsa_w_out, final_norm):
    bsz, s, d = x.shape
    depth = ffn1_norm.shape[0]
    h = x.reshape(bsz * s, d)
    w1_gu, w1_down = ffn1_w_gu.astype(BF16), ffn1_w_down.astype(BF16)
    w2_gu, w2_down = ffn2_w_gu.astype(BF16), ffn2_w_down.astype(BF16)
    for i in range(depth):
        j = i // 2
        h = _ffn(h, [], ffn1_norm[i], w1_gu, w1_down, final_norm, layer=i, final=False)
        if i % 2 == 0:
            mix = _ab_mixer(h, mix_norm[i], ab_w_in[j], rg_conv_w[j], rg_conv_b[j], rg_w_r[j], rg_b_r[j],
                            rg_w_i[j], rg_b_i[j], rg_lambda[j], ml_conv_w[j], ml_conv_b[j], ml_b_i[j],
                            ml_b_f[j], ml_norm[j], ab_w_out[j], bsz, s)
        else:
            mix = _nsa_mixer(h, mix_norm[i], nsa_w_in[j], nsa_pe_k[j], nsa_k_w1[j], nsa_k_b1[j],
                             nsa_k_w2[j], nsa_pe_v[j], nsa_v_w1[j], nsa_v_b1[j], nsa_v_w2[j],
                             nsa_b_gate[j], nsa_w_out[j], bsz, s)
        h = _ffn(h, mix, ffn2_norm[i], w2_gu, w2_down, final_norm, layer=i, final=(i == depth - 1))
    return h.reshape(bsz, s, d)
```

```python
import functools

import jax
import jax.numpy as jnp
import numpy as np
from jax import lax
from jax.experimental import pallas as pl
from jax.experimental.pallas import tpu as pltpu

F32 = jnp.float32
BF16 = jnp.bfloat16
HIGHEST = lax.Precision.HIGHEST

EPS = 1e-6
RG_C = 8.0
RG_BLOCKS = 8
CONV_WIDTH = 4
ML_HEADS = 4
ML_CHUNK = 128
NSA_HEADS = 16
NSA_GROUPS = 4
NSA_HEAD_DIM = 64
NSA_BRANCHES = 3
CMP_BLOCK = 32
CMP_STRIDE = 16
SEL_BLOCK = 64
SEL_TOPN = 8
WINDOW = 256
FORCED_SCORE = 1e6

LANES = 128
SUBLANES = 8
MXU_TILE = 256
NEG = -1e30
TAKEN = -2.0
VMEM_LIMIT = 48 * 1024 * 1024

TOKEN_TILE = 512
NSA_QTILE = 256
NSA_VROWS = 80
NSA_STEP_GROUPS = 4
LOG2E = 1.4426950408889634
NT_DIMS = (((1,), (1,)), ((), ()))


def _params(n_axes):
    return pltpu.CompilerParams(dimension_semantics=("arbitrary",) * n_axes,
                                vmem_limit_bytes=VMEM_LIMIT)


def _resident(shape, index_map):
    return pl.BlockSpec(shape, index_map, pipeline_mode=pl.Buffered(1))


def _rms(x, g):
    return x * lax.rsqrt(jnp.mean(x * x, axis=-1, keepdims=True) + EPS) * g


def _gelu_tanh(x):
    return 0.5 * x * (1.0 + jnp.tanh(0.7978845608028654 * (x + 0.044715 * (x * x * x))))


def _softplus(z):
    return jnp.maximum(z, 0.0) + jnp.log1p(jnp.exp(-jnp.abs(z)))


def _sigmoid(z):
    return 0.5 * jnp.tanh(0.5 * z) + 0.5


def _sqrt_nonneg(z):
    return jnp.where(z > 0.0, z * lax.rsqrt(z), 0.0)


def _ffn_kernel(*refs, n_mix, n_chunks, final):
    x_ref = refs[0]
    a_refs = refs[1:1 + n_mix]
    wo_refs = refs[1 + n_mix:1 + 2 * n_mix]
    g_ref, wg_ref, wu_ref, wd_ref, gf_ref, o_ref = refs[1 + 2 * n_mix:]
    x = x_ref[...]
    for a_ref, wo_ref in zip(a_refs, wo_refs):
        x = x + jnp.dot(a_ref[...], wo_ref[...], preferred_element_type=F32)
    xn = _rms(x, g_ref[...]).astype(BF16)
    n_tiles = wg_ref.shape[1] // MXU_TILE
    bounds = [MXU_TILE * ((n_tiles * c + n_chunks - 1) // n_chunks) for c in range(n_chunks + 1)]
    acc = None
    for lo, hi in zip(bounds[:-1], bounds[1:]):
        gate = jnp.dot(xn, wg_ref[:, lo:hi], preferred_element_type=F32)
        up = jnp.dot(xn, wu_ref[:, lo:hi], preferred_element_type=F32)
        h = (gate * jax.nn.sigmoid(gate) * up).astype(BF16)
        part = jnp.dot(h, wd_ref[lo:hi, :], preferred_element_type=F32)
        acc = part if acc is None else acc + part
    y = x + 0.5 * acc
    if final:
        y = _rms(y, gf_ref[...])
    o_ref[...] = y


def _ffn(x, mix, g, w_gu, w_down, gf, *, layer, final):
    n, d = x.shape
    f = w_down.shape[1]
    tm = min(TOKEN_TILE, n)
    const = lambda i: (0, 0)
    row = lambda i: (i, 0)
    acts = [a for a, _ in mix]
    wos = [w for _, w in mix]
    return pl.pallas_call(
        functools.partial(_ffn_kernel, n_mix=len(mix), n_chunks=2, final=final),
        grid=(n // tm,),
        in_specs=[pl.BlockSpec((tm, d), row)]
        + [pl.BlockSpec((tm, a.shape[1]), row) for a in acts]
        + [_resident(w.shape, const) for w in wos]
        + [pl.BlockSpec((1, d), const),
           _resident((None, d, f), lambda i: (layer, 0, 0)),
           _resident((None, d, f), lambda i: (layer, 0, 1)),
           _resident((None, f, d), lambda i: (layer, 0, 0)),
           pl.BlockSpec((1, d), const)],
        out_specs=pl.BlockSpec((tm, d), row),
        out_shape=jax.ShapeDtypeStruct((n, d), F32),
        compiler_params=_params(1),
        name="ffn_final" if final else ("ffn_mix" if mix else "ffn"),
    )(x, *acts, *wos, g.reshape(1, d), w_gu, w_gu, w_down, gf.reshape(1, d))


def _norm_matmul_kernel(x_ref, g_ref, w_ref, o_ref):
    xn = _rms(x_ref[...], g_ref[...]).astype(BF16)
    o_ref[...] = jnp.dot(xn, w_ref[...], preferred_element_type=F32)


def _norm_matmul(x, g, w):
    n, d = x.shape
    dout = w.shape[1]
    tm = min(TOKEN_TILE, n)
    const = lambda i: (0, 0)
    return pl.pallas_call(
        _norm_matmul_kernel,
        grid=(n // tm,),
        in_specs=[pl.BlockSpec((tm, d), lambda i: (i, 0)),
                  pl.BlockSpec((1, d), const),
                  _resident((d, dout), const)],
        out_specs=pl.BlockSpec((tm, dout), lambda i: (i, 0)),
        out_shape=jax.ShapeDtypeStruct((n, dout), F32),
        compiler_params=_params(1),
        name="norm_matmul",
    )(x, g.reshape(1, d), w)


def _causal_conv(x, cw, cb, pad_ref):
    s = x.shape[0]
    pad_ref[0:SUBLANES, :] = jnp.zeros((SUBLANES, x.shape[1]), F32)
    pad_ref[SUBLANES:SUBLANES + s, :] = x
    y = cb + cw[CONV_WIDTH - 1:CONV_WIDTH, :] * x
    for j in range(CONV_WIDTH - 1):
        off = SUBLANES - (CONV_WIDTH - 1 - j)
        y = y + cw[j:j + 1, :] * pad_ref[off:off + s, :]
    return y


def _causal_conv_blocks(x, cw, cb, pad_ref, emit, block=256):
    s = x.shape[0]
    pad_ref[0:SUBLANES, :] = jnp.zeros((SUBLANES, x.shape[1]), F32)
    pad_ref[SUBLANES:SUBLANES + s, :] = x
    for r0 in range(0, s, block):
        y = cb + cw[CONV_WIDTH - 1:CONV_WIDTH, :] * pad_ref[SUBLANES + r0:SUBLANES + r0 + block, :]
        for j in range(CONV_WIDTH - 1):
            off = r0 + SUBLANES - (CONV_WIDTH - 1 - j)
            y = y + cw[j:j + 1, :] * pad_ref[off:off + block, :]
        emit(r0, y)


def _rglru_kernel(xa_ref, ga_ref, cw_ref, cb_ref, wr_ref, br_ref, wi_ref, bi_ref, lam_ref,
                  o_ref, pad_ref, a_ref, u_ref):
    s = xa_ref.shape[1]
    ng = s // SUBLANES
    xc = _causal_conv(xa_ref[0], cw_ref[...], cb_ref[...], pad_ref)
    xb = xc.astype(BF16)
    r = _sigmoid(jnp.dot(xb, wr_ref[...], preferred_element_type=F32) + br_ref[...])
    ig = _sigmoid(jnp.dot(xb, wi_ref[...], preferred_element_type=F32) + bi_ref[...])
    log_a = -RG_C * r * _softplus(-lam_ref[...])
    a = jnp.exp(log_a)
    th = jnp.tanh(log_a)
    u = _sqrt_nonneg(-2.0 * th / (1.0 - th)) * (ig * xc)
    a3 = a.reshape(ng, SUBLANES, LANES)
    u3 = u.reshape(ng, SUBLANES, LANES)
    row = lax.broadcasted_iota(jnp.int32, (ng, SUBLANES, LANES), 1)
    sh = 1
    while sh < SUBLANES:
        a_s = pltpu.roll(a3, sh, 1)
        u_s = pltpu.roll(u3, sh, 1)
        m = row >= sh
        u3 = jnp.where(m, a3 * u_s + u3, u3)
        a3 = jnp.where(m, a3 * a_s, a3)
        sh *= 2
    a_ref[...] = a3
    u_ref[...] = u3

    def body(i, h):
        hh = a_ref[i] * h + u_ref[i]
        u_ref[i] = hh
        return hh[SUBLANES - 1:SUBLANES, :]

    lax.fori_loop(0, ng, body, jnp.zeros((1, LANES), F32), unroll=True)
    o_ref[0] = (_gelu_tanh(ga_ref[0]) * u_ref[...].reshape(s, LANES)).astype(o_ref.dtype)


def _rglru(proj3, cw, cb, wr_bd, br, wi_bd, bi, lam):
    bsz, s, _ = proj3.shape
    c = cw.shape[1]
    nb = c // LANES
    vec = lambda b, j: (0, j)
    return pl.pallas_call(
        _rglru_kernel,
        grid=(bsz, nb),
        in_specs=[pl.BlockSpec((1, s, LANES), lambda b, j: (b, 0, j)),
                  pl.BlockSpec((1, s, LANES), lambda b, j: (b, 0, nb + j)),
                  pl.BlockSpec((CONV_WIDTH, LANES), vec),
                  pl.BlockSpec((1, LANES), vec),
                  pl.BlockSpec((LANES, LANES), lambda b, j: (j, j)),
                  pl.BlockSpec((1, LANES), vec),
                  pl.BlockSpec((LANES, LANES), lambda b, j: (j, j)),
                  pl.BlockSpec((1, LANES), vec),
                  pl.BlockSpec((1, LANES), vec)],
        out_specs=pl.BlockSpec((1, s, LANES), lambda b, j: (b, 0, j)),
        out_shape=jax.ShapeDtypeStruct((bsz, s, c), BF16),
        scratch_shapes=[pltpu.VMEM((s + SUBLANES, LANES), F32),
                        pltpu.VMEM((s // SUBLANES, SUBLANES, LANES), F32),
                        pltpu.VMEM((s // SUBLANES, SUBLANES, LANES), F32)],
        compiler_params=_params(2),
        name="rglru",
    )(proj3, proj3, cw, cb.reshape(1, c), wr_bd, br.reshape(1, c), wi_bd, bi.reshape(1, c),
      lam.reshape(1, c))


def _mlstm_kernel(bias_ref, q_ref, k_ref, v_ref, og_ref, gi_ref, gf_ref, cwq_ref, cbq_ref,
                  cwk_ref, cbk_ref, ng_ref, o_ref, pad_ref, q_sc, k_sc, li_sc, lf_sc, bc_sc):
    hd = pl.program_id(1)
    s = q_ref.shape[1]
    dh = q_ref.shape[2]
    nc = s // ML_CHUNK

    def put_q(r0, y):
        q_sc[r0:r0 + y.shape[0], :] = jax.nn.silu(y) * (dh ** -0.5)

    def put_k(r0, y):
        k_sc[r0:r0 + y.shape[0], :] = jax.nn.silu(y)

    _causal_conv_blocks(q_ref[0], cwq_ref[...], cbq_ref[...], pad_ref, put_q)
    _causal_conv_blocks(k_ref[0], cwk_ref[...], cbk_ref[...], pad_ref, put_k)
    li_sc[...] = gi_ref[0, 0] + bias_ref[hd]
    lf = -_softplus(-(gf_ref[0, 0] + bias_ref[ML_HEADS + hd]))
    lf_sc[...] = lf
    jj = lax.broadcasted_iota(jnp.int32, (ML_CHUNK, ML_CHUNK), 0)
    kk = lax.broadcasted_iota(jnp.int32, (ML_CHUNK, ML_CHUNK), 1)
    tri = kk <= jj
    upper = jnp.where(jj <= kk, 1.0, 0.0).astype(F32)
    bc_sc[...] = jnp.dot(lf, upper, precision=HIGHEST, preferred_element_type=F32)
    norm_g = ng_ref[...]

    def chunk(c, carry):
        c_st, n_st, m_st = carry
        r0 = pl.multiple_of(c * ML_CHUNK, ML_CHUNK)
        qc = q_sc[pl.ds(r0, ML_CHUNK), :]
        kc = k_sc[pl.ds(r0, ML_CHUNK), :]
        vc = v_ref[0, pl.ds(r0, ML_CHUNK), :]
        li_row = li_sc[pl.ds(c, 1), :]
        lf_row = lf_sc[pl.ds(c, 1), :]
        b_row = bc_sc[pl.ds(c, 1), :]
        b_col = jnp.sum(jnp.where(tri, lf_row, 0.0), axis=1, keepdims=True)
        g_tot = jnp.sum(lf_row, axis=1, keepdims=True)
        w_row = g_tot - b_row + li_row
        m_loc = jnp.max(w_row, axis=1, keepdims=True)
        wk = jnp.exp(w_row - m_loc)
        lhs = jnp.concatenate([vc.T * wk, jnp.broadcast_to(wk, (2 * SUBLANES, ML_CHUNK))], axis=0)
        cn = jnp.dot(lhs.astype(BF16), kc.astype(BF16), preferred_element_type=F32)
        c_loc = cn[0:dh]
        n_loc = cn[dh:dh + 1]
        d = jnp.where(tri, b_col - b_row + li_row, NEG)
        m_inter = b_col + m_st
        m = jnp.maximum(m_inter, jnp.max(d, axis=1, keepdims=True))
        qb = qc.astype(BF16)
        qk = lax.dot_general(qb, kc.astype(BF16), NT_DIMS, preferred_element_type=F32)
        p = jnp.exp(d - m) * qk
        sc = jnp.exp(m_inter - m)
        inter = lax.dot_general(qb, c_st.astype(BF16), NT_DIMS, preferred_element_type=F32)
        num = sc * inter + jnp.dot(p.astype(BF16), vc.astype(BF16), preferred_element_type=F32)
        den = jnp.sum(sc * (qc * n_st) + p, axis=1, keepdims=True)
        h = num / jnp.maximum(jnp.abs(den), jnp.exp(-m))
        h = h * lax.rsqrt(jnp.mean(h * h, axis=1, keepdims=True) + EPS) * norm_g
        o_ref[0, pl.ds(r0, ML_CHUNK), :] = (
            jax.nn.sigmoid(og_ref[0, pl.ds(r0, ML_CHUNK), :]) * h).astype(o_ref.dtype)
        m_new = jnp.maximum(g_tot + m_st, m_loc)
        sa = jnp.exp(g_tot + m_st - m_new)
        sb = jnp.exp(m_loc - m_new)
        return sa * c_st + sb * c_loc, sa * n_st + sb * n_loc, m_new

    init = (jnp.zeros((dh, dh), F32), jnp.zeros((1, dh), F32), jnp.full((1, 1), NEG, F32))
    lax.fori_loop(0, nc, chunk, init, unroll=True)


def _mlstm(proj3, gates_t, bias, cw, cb, norm_g, *, col0):
    bsz, s, _ = proj3.shape
    nh = ML_HEADS
    dh = LANES
    nc = s // ML_CHUNK
    base = col0 // dh
    blk = lambda off: pl.BlockSpec((1, s, dh), lambda b, h: (b, 0, base + off + h))
    vec = lambda off: (lambda b, h: (0, off + h))
    return pl.pallas_call(
        _mlstm_kernel,
        grid=(bsz, nh),
        in_specs=[pl.BlockSpec(memory_space=pltpu.SMEM),
                  blk(0), blk(nh), blk(2 * nh), blk(3 * nh),
                  pl.BlockSpec((1, 1, nc, ML_CHUNK), lambda b, h: (b, h, 0, 0)),
                  pl.BlockSpec((1, 1, nc, ML_CHUNK), lambda b, h: (b, nh + h, 0, 0)),
                  pl.BlockSpec((CONV_WIDTH, dh), vec(0)), pl.BlockSpec((1, dh), vec(0)),
                  pl.BlockSpec((CONV_WIDTH, dh), vec(nh)), pl.BlockSpec((1, dh), vec(nh)),
                  pl.BlockSpec((1, dh), vec(0))],
        out_specs=pl.BlockSpec((1, s, dh), lambda b, h: (b, 0, h)),
        out_shape=jax.ShapeDtypeStruct((bsz, s, nh * dh), BF16),
        scratch_shapes=[pltpu.VMEM((s + SUBLANES, dh), F32),
                        pltpu.VMEM((s, dh), F32), pltpu.VMEM((s, dh), F32),
                        pltpu.VMEM((nc, ML_CHUNK), F32), pltpu.VMEM((nc, ML_CHUNK), F32),
                        pltpu.VMEM((nc, ML_CHUNK), F32)],
        compiler_params=_params(2),
        name="mlstm",
    )(bias, proj3, proj3, proj3, proj3, gates_t, gates_t, cw, cb.reshape(1, -1), cw,
      cb.reshape(1, -1), norm_g.reshape(1, -1))


def _ab_mixer(x, g, w_in, rg_conv_w, rg_conv_b, rg_w_r, rg_b_r, rg_w_i, rg_b_i, rg_lambda,
              ml_conv_w, ml_conv_b, ml_b_i, ml_b_f, ml_norm, w_out, bsz, s):
    d_rg = rg_conv_w.shape[1]
    d_ml = ml_norm.shape[0]
    d_main = 2 * d_rg + 4 * d_ml
    n_gate = 2 * ML_HEADS
    w_in_p = jnp.pad(w_in, ((0, 0), (0, LANES - n_gate))).astype(BF16)
    proj = _norm_matmul(x, g, w_in_p)
    proj3 = proj.reshape(bsz, s, proj.shape[1])
    bd = lambda w: jax.scipy.linalg.block_diag(*[w[i] for i in range(RG_BLOCKS)]).astype(BF16)
    ya = _rglru(proj3, rg_conv_w, rg_conv_b, bd(rg_w_r), rg_b_r, bd(rg_w_i), rg_b_i, rg_lambda)
    gates_t = proj3[:, :, d_main:d_main + n_gate].transpose(0, 2, 1).reshape(
        bsz, n_gate, s // ML_CHUNK, ML_CHUNK)
    yb = _mlstm(proj3, gates_t, jnp.concatenate([ml_b_i, ml_b_f]), ml_conv_w, ml_conv_b, ml_norm,
                col0=2 * d_rg)
    w_out_b = w_out.astype(BF16)
    return [(ya.reshape(bsz * s, d_rg), w_out_b[:d_rg]), (yb.reshape(bsz * s, d_ml), w_out_b[d_rg:])]


def _nsa_proj_kernel(x_ref, g_ref, w_ref, q_ref, gt_ref, kc_ref, vc_ref, ks_ref, kw_ref, vs_ref, vw_ref,
                     *, seq):
    tm = x_ref.shape[0]
    kv = NSA_GROUPS * NSA_HEAD_DIM
    width = NSA_HEADS * NSA_HEAD_DIM
    half = LANES // 2
    xn = _rms(x_ref[...], g_ref[...]).astype(BF16)
    proj = jnp.dot(xn, w_ref[...], preferred_element_type=F32)
    lane = lax.broadcasted_iota(jnp.int32, (tm, LANES), 1)
    lo = lane < half

    def lane_pair(c0):
        p = proj[:, c0:c0 + LANES]
        return p, pltpu.roll(p, half, 1)

    scale = NSA_HEAD_DIM ** -0.5 * LOG2E
    for pair in range(NSA_HEADS // 2):
        for j, piece in enumerate(lane_pair(pair * LANES)):
            h = 2 * pair + j
            q_ref[:, h * LANES:(h + 1) * LANES] = jnp.where(lo, piece * scale, 0.0).astype(BF16)
    gt_ref[0] = proj[:, width + 6 * kv:width + 6 * kv + LANES].T
    kc_ref[...] = proj[:, width:width + kv]
    vc_ref[...] = proj[:, width + kv:width + 2 * kv]
    pos = (pl.program_id(0) % (seq // tm)) * tm + lax.broadcasted_iota(jnp.int32, (tm, LANES), 0)
    onehot = jnp.where(lane == half + pos // SEL_BLOCK, 1.0, 0.0)
    for pair in range(NSA_GROUPS // 2):
        for idx, ref, fill in ((2, ks_ref, onehot), (4, kw_ref, 0.0)):
            for j, piece in enumerate(lane_pair(width + idx * kv + pair * LANES)):
                ref[0, 2 * pair + j] = jnp.where(lo, piece, fill).astype(BF16)
        for idx, ref in ((3, vs_ref), (5, vw_ref)):
            c0 = width + idx * kv + pair * LANES
            t = proj[:, c0:c0 + LANES].T.astype(BF16)
            ones = jnp.ones((NSA_VROWS - half, NSA_QTILE), BF16)
            for j in range(tm // NSA_QTILE):
                for hf in range(2):
                    ref[0, 2 * pair + hf, j, 0:half, :] = t[hf * half:(hf + 1) * half,
                                                            j * NSA_QTILE:(j + 1) * NSA_QTILE]
                    ref[0, 2 * pair + hf, j, half:NSA_VROWS, :] = ones


def _nsa_proj(x, g, w, bsz, s):
    n, d = x.shape
    dout = w.shape[1]
    tm = min(TOKEN_TILE, s)
    nsb = s // tm
    ng, dk = NSA_GROUPS, NSA_HEAD_DIM
    kv = ng * dk
    tq = NSA_QTILE
    const = lambda i: (0, 0)
    row = lambda i: (i, 0)
    keyed = lambda i: (i // nsb, 0, i % nsb, 0)
    return pl.pallas_call(
        functools.partial(_nsa_proj_kernel, seq=s),
        grid=(n // tm,),
        in_specs=[pl.BlockSpec((tm, d), row), pl.BlockSpec((1, d), const), _resident((d, dout), const)],
        out_specs=[pl.BlockSpec((tm, NSA_HEADS * LANES), row),
                   pl.BlockSpec((1, LANES, tm), lambda i: (i // nsb, 0, i % nsb)),
                   pl.BlockSpec((tm, kv), row), pl.BlockSpec((tm, kv), row),
                   pl.BlockSpec((1, ng, tm, LANES), keyed), pl.BlockSpec((1, ng, tm, LANES), keyed),
                   pl.BlockSpec((1, ng, tm // tq, NSA_VROWS, tq), lambda i: (i // nsb, 0, i % nsb, 0, 0)),
                   pl.BlockSpec((1, ng, tm // tq, NSA_VROWS, tq), lambda i: (i // nsb, 0, i % nsb, 0, 0))],
        out_shape=[jax.ShapeDtypeStruct((n, NSA_HEADS * LANES), BF16),
                   jax.ShapeDtypeStruct((bsz, LANES, s), F32),
                   jax.ShapeDtypeStruct((n, kv), F32), jax.ShapeDtypeStruct((n, kv), F32),
                   jax.ShapeDtypeStruct((bsz, ng, s, LANES), BF16),
                   jax.ShapeDtypeStruct((bsz, ng, s, LANES), BF16),
                   jax.ShapeDtypeStruct((bsz, ng, s // tq, NSA_VROWS, tq), BF16),
                   jax.ShapeDtypeStruct((bsz, ng, s // tq, NSA_VROWS, tq), BF16)],
        compiler_params=_params(1),
        name="nsa_proj",
    )(x, g.reshape(1, d), w)


def _compress_kernel(x_ref, pe_ref, w1_ref, b1_ref, w2_ref, o_ref, *, transpose_out):
    nb = x_ref.shape[1] // CMP_STRIDE
    top = bot = None
    for l in range(CMP_STRIDE):
        rows = x_ref[0, pl.ds(l, nb, stride=CMP_STRIDE), :]
        t = jnp.dot((rows + pe_ref[l:l + 1, :]).astype(BF16), w1_ref[l], preferred_element_type=F32)
        b = jnp.dot((rows + pe_ref[CMP_STRIDE + l:CMP_STRIDE + l + 1, :]).astype(BF16),
                    w1_ref[CMP_STRIDE + l], preferred_element_type=F32)
        top = t if top is None else top + t
        bot = b if bot is None else bot + b
    pre = top + pltpu.roll(bot, nb - 1, 0) + b1_ref[...]
    out = jnp.dot(_gelu_tanh(pre).astype(BF16), w2_ref[...], preferred_element_type=F32)
    out = jnp.where(lax.broadcasted_iota(jnp.int32, out.shape, 0) < nb - 1, out, 0.0)
    for hf in range(2):
        piece = out[:, hf * LANES:(hf + 1) * LANES]
        if transpose_out:
            o_ref[0, hf] = piece.T[0:NSA_HEAD_DIM, :].astype(o_ref.dtype)
        else:
            o_ref[0, hf] = piece.astype(o_ref.dtype)


def _compress(kv_in, pe, w1, b1, w2, *, transpose_out):
    bsz, s, _ = kv_in.shape
    dk, hid = NSA_HEAD_DIM, w1.shape[1]
    nb = s // CMP_STRIDE
    w1r = w1.reshape(CMP_BLOCK, dk, hid)
    z = jnp.zeros_like(w1r)
    w1p = jnp.concatenate([jnp.concatenate([w1r, z], axis=2), jnp.concatenate([z, w1r], axis=2)],
                          axis=1).astype(BF16)
    w2w = jnp.pad(w2, ((0, 0), (0, LANES - dk)))
    z2 = jnp.zeros_like(w2w)
    w2p = jnp.concatenate([jnp.concatenate([w2w, z2], axis=1), jnp.concatenate([z2, w2w], axis=1)],
                          axis=0).astype(BF16)
    const2 = lambda b, p: (0, 0)
    out_block = (1, 2, dk, nb) if transpose_out else (1, 2, nb, LANES)
    out_shape = (bsz, NSA_GROUPS, dk, nb) if transpose_out else (bsz, NSA_GROUPS, nb, LANES)
    return pl.pallas_call(
        functools.partial(_compress_kernel, transpose_out=transpose_out),
        grid=(bsz, NSA_GROUPS // 2),
        in_specs=[pl.BlockSpec((1, s, LANES), lambda b, p: (b, 0, p)),
                  pl.BlockSpec((CMP_BLOCK, LANES), const2),
                  _resident((CMP_BLOCK, LANES, 2 * hid), lambda b, p: (0, 0, 0)),
                  pl.BlockSpec((1, 2 * hid), const2),
                  _resident((2 * hid, 2 * LANES), const2)],
        out_specs=pl.BlockSpec(out_block, lambda b, p: (b, p, 0, 0)),
        out_shape=jax.ShapeDtypeStruct(out_shape, BF16),
        compiler_params=_params(2),
        name="nsa_compress",
    )(kv_in, jnp.concatenate([pe, pe], axis=1), w1p, jnp.concatenate([b1, b1]).reshape(1, 2 * hid), w2p)


def _nsa_kernel(bg_ref, q_ref, gt_ref, kc_ref, vc_ref, ks_ref, vs_ref, kw_ref, vw_ref, msel_ref,
                o_ref, qa_sc, s_sc, bias_sc, m_sc, acc_sc, part_sc):
    gp = pl.program_id(1)
    qi = pl.program_id(2)
    tq = q_ref.shape[0]
    rep = NSA_HEADS // NSA_GROUPS
    dk = NSA_HEAD_DIM
    half = LANES // 2
    n_sel = 32
    units = range(NSA_STEP_GROUPS)
    heads = range(rep)
    s0 = qi * tq
    key_off = lax.broadcasted_iota(jnp.int32, (tq, tq), 0)
    qry_off = lax.broadcasted_iota(jnp.int32, (tq, tq), 1)
    cols = lambda r: slice(r * tq, (r + 1) * tq)
    nt_dot = lambda k, q: lax.dot_general(k, q, NT_DIMS, preferred_element_type=F32)
    q_head = lambda u, r: q_ref[:, (u * rep + r) * LANES:(u * rep + r + 1) * LANES]

    def gate_row(u, r, br):
        idx = (gp * NSA_STEP_GROUPS + u) * (rep * NSA_BRANCHES) + r * NSA_BRANCHES + br
        return jax.nn.sigmoid(gt_ref[0, pl.ds(idx, 1), :] + bg_ref[idx])

    kp = jnp.maximum(qi - 1, 0)
    p0 = pl.multiple_of(kp * tq, tq)
    d0 = pl.multiple_of(qi * tq, tq)
    cmp_scores = [[nt_dot(kc_ref[0, u], q_head(u, r)) for r in heads] for u in units]
    win_scores = [[(nt_dot(kw_ref[0, u, pl.ds(p0, tq), :], q_head(u, r)),
                    nt_dot(kw_ref[0, u, pl.ds(d0, tq), :], q_head(u, r))) for r in heads] for u in units]
    for u in units:
        for r in heads:
            s_sc[1, u, r] = nt_dot(ks_ref[0, u, pl.ds(d0, tq), :], q_head(u, r))

    n_blk = kc_ref.shape[2]
    blk_id = lax.broadcasted_iota(jnp.int32, (n_blk, tq), 0)
    t_blk = s0 + lax.broadcasted_iota(jnp.int32, (n_blk, tq), 1)
    cmask = (blk_id * CMP_STRIDE + (CMP_BLOCK - 1) <= t_blk) & (blk_id < n_blk - 1)
    psum = [None] * NSA_STEP_GROUPS
    for r in heads:
        for u in units:
            scm = jnp.where(cmask, cmp_scores[u][r], NEG)
            e = jnp.where(cmask, jnp.exp2(scm - jnp.max(scm, axis=0, keepdims=True)), 0.0)
            den = jnp.sum(e, axis=0, keepdims=True)
            p = e * (1.0 / jnp.where(den > 0.0, den, 1.0))
            psum[u] = p if psum[u] is None else psum[u] + p
            part_sc[u, :, cols(r)] = gate_row(u, r, 0) * jnp.dot(vc_ref[0, u], p.astype(BF16),
                                                                 preferred_element_type=F32)

    jb = lax.broadcasted_iota(jnp.int32, (n_sel, tq), 0)
    jbf = jb.astype(F32)
    cur = (s0 + lax.broadcasted_iota(jnp.int32, (n_sel, tq), 1)) // SEL_BLOCK
    valid = jb <= cur
    forced = ((jb == 0) | (jb == cur) | (jb == cur - 1)) & valid
    for u in units:
        p_hi = psum[u].astype(BF16)
        p_lo = (psum[u] - p_hi.astype(F32)).astype(BF16)
        score_t = (jnp.dot(msel_ref[...], p_hi, preferred_element_type=F32)
                   + jnp.dot(msel_ref[...], p_lo, preferred_element_type=F32))
        s2 = jnp.where(forced, FORCED_SCORE, jnp.where(valid, score_t[half:half + n_sel, :], -1.0))
        for _ in range(SEL_TOPN):
            best = jnp.max(s2, axis=0, keepdims=True)
            first = jnp.min(jnp.where(s2 == best, jbf, float(n_sel)), axis=0, keepdims=True)
            s2 = jnp.where(jbf == first, TAKEN, s2)
        bias_t = jnp.where((s2 == TAKEN) & valid, 0.0, NEG)
        bias_sc[u] = bias_t
        bias_full = jnp.concatenate(
            [jnp.zeros((half, tq), F32), bias_t, jnp.zeros((LANES - half - n_sel, tq), F32)], axis=0)
        bias_q = bias_full.T
        for r in heads:
            qa_sc[u, r] = (q_head(u, r).astype(F32) + bias_q).astype(BF16)

    prev_mask = (key_off > qry_off) & (qi >= 1)
    diag_mask = key_off <= qry_off
    for r in heads:
        for u in units:
            s_prev = jnp.where(prev_mask, win_scores[u][r][0], NEG)
            s_diag = jnp.where(diag_mask, win_scores[u][r][1], NEG)
            mw = jnp.maximum(jnp.max(s_prev, axis=0, keepdims=True), jnp.max(s_diag, axis=0, keepdims=True))
            win = (jnp.dot(vw_ref[0, u, kp], jnp.exp2(s_prev - mw).astype(BF16), preferred_element_type=F32)
                   + jnp.dot(vw_ref[0, u, qi], jnp.exp2(s_diag - mw).astype(BF16),
                             preferred_element_type=F32))
            part_sc[u, :, cols(r)] += gate_row(u, r, 2) * (win[0:dk] * (1.0 / win[dk:dk + 1]))

    def qk_into(kt, slot):
        k0 = pl.multiple_of(kt * tq, tq)
        for u in units:
            k_tile = ks_ref[0, u, pl.ds(k0, tq), :]
            for r in heads:
                s_sc[slot, u, r] = nt_dot(k_tile, qa_sc[u, r])

    qk_into(0, 0)
    blk0 = qi * (tq // SEL_BLOCK)
    for u in units:
        bias_diag = jnp.concatenate(
            [jnp.broadcast_to(bias_sc[u, pl.ds(blk0 + jj, 1), :], (SEL_BLOCK, tq))
             for jj in range(tq // SEL_BLOCK)], axis=0)
        for r in heads:
            s = jnp.where(diag_mask, s_sc[1, u, r] + bias_diag, NEG)
            m0 = jnp.max(s, axis=0, keepdims=True)
            m_sc[u, :, cols(r)] = m0
            acc_sc[u, :, cols(r)] = jnp.dot(vs_ref[0, u, qi], jnp.exp2(s - m0).astype(BF16),
                                            preferred_element_type=F32)

    def softmax_pv(kt, slot):
        for u in units:
            v_t = vs_ref[0, u, kt]
            for r in heads:
                s = s_sc[slot, u, r]
                m_old = m_sc[u, :, cols(r)]
                m_new = jnp.maximum(m_old, jnp.max(s, axis=0, keepdims=True))
                alpha = jnp.exp2(m_old - m_new)
                p = jnp.exp2(s - m_new)
                acc_sc[u, :, cols(r)] = alpha * acc_sc[u, :, cols(r)] + jnp.dot(
                    v_t, p.astype(BF16), preferred_element_type=F32)
                m_sc[u, :, cols(r)] = m_new


    def pair_body(j, carry):
        a = 2 * j
        qk_into(a + 1, 1)
        softmax_pv(a, 0)
        qk_into(a + 2, 0)
        softmax_pv(a + 1, 1)
        return carry

    lax.fori_loop(0, qi // 2, pair_body, 0)

    @pl.when(qi % 2 == 1)
    def _():
        softmax_pv(qi - 1, 0)

    for u in units:
        merged = []
        for r in heads:
            o_slc = acc_sc[u, 0:dk, cols(r)] * (1.0 / acc_sc[u, dk:dk + 1, cols(r)])
            merged.append(part_sc[u, :, cols(r)] + gate_row(u, r, 1) * o_slc)
        for pair in range(rep // 2):
            both = jnp.concatenate([merged[2 * pair], merged[2 * pair + 1]], axis=0)
            c0 = (u * (rep // 2) + pair) * LANES
            o_ref[:, c0:c0 + LANES] = both.T.astype(o_ref.dtype)


def _nsa_attention(q, gt, b_gate, kc, vc_t, ks, vs_t, kw, vw_t, msel_t, bsz, s):
    n = q.shape[0]
    tq = NSA_QTILE
    assert tq == WINDOW and s % tq == 0 and s // SEL_BLOCK == 32
    rep = NSA_HEADS // NSA_GROUPS
    dk = NSA_HEAD_DIM
    nq = s // tq
    ug = NSA_STEP_GROUPS
    per_bg = lambda a: pl.BlockSpec((1, ug) + a.shape[2:], lambda b, g, i: (b, g) + (0,) * (a.ndim - 2))
    return pl.pallas_call(
        _nsa_kernel,
        grid=(bsz, NSA_GROUPS // ug, nq),
        in_specs=[pl.BlockSpec(memory_space=pltpu.SMEM),
                  pl.BlockSpec((tq, ug * rep * LANES), lambda b, g, i: (b * nq + i, g)),
                  pl.BlockSpec((1, LANES, tq), lambda b, g, i: (b, 0, i)),
                  per_bg(kc), per_bg(vc_t), per_bg(ks), per_bg(vs_t), per_bg(kw), per_bg(vw_t),
                  pl.BlockSpec(msel_t.shape, lambda b, g, i: (0, 0))],
        out_specs=pl.BlockSpec((tq, ug * rep * dk), lambda b, g, i: (b * nq + i, g)),
        out_shape=jax.ShapeDtypeStruct((n, NSA_HEADS * dk), BF16),
        scratch_shapes=[pltpu.VMEM((ug, rep, tq, LANES), BF16),
                        pltpu.VMEM((2, ug, rep, tq, tq), F32),
                        pltpu.VMEM((ug, s // SEL_BLOCK, tq), F32),
                        pltpu.VMEM((ug, 1, rep * tq), F32),
                        pltpu.VMEM((ug, NSA_VROWS, rep * tq), F32),
                        pltpu.VMEM((ug, dk, rep * tq), F32)],
        compiler_params=_params(3),
        name="nsa_attention",
    )(b_gate, q, gt, kc, vc_t, ks, vs_t, kw, vw_t, msel_t)


def _sel_matrix_t(n_rows, n_sel):
    n_cmp = np.arange(n_rows)[None, :] * CMP_STRIDE
    sel = np.arange(n_sel)[:, None] * SEL_BLOCK
    ov = np.clip(np.minimum(n_cmp + CMP_BLOCK, sel + SEL_BLOCK) - np.maximum(n_cmp, sel), 0, None)
    out = np.zeros((LANES, n_rows), np.float32)
    out[LANES // 2:LANES // 2 + n_sel] = ov / CMP_BLOCK
    return jnp.asarray(out, dtype=BF16)


def _nsa_mixer(x, g, w_in, pe_k, k_w1, k_b1, k_w2, pe_v, v_w1, v_b1, v_w2, b_gate, w_out, bsz, s):
    n_gate = NSA_BRANCHES * NSA_HEADS
    kv = NSA_GROUPS * NSA_HEAD_DIM
    w_in_p = jnp.pad(w_in, ((0, 0), (0, LANES - n_gate))).astype(BF16)
    q, gt, kc_in, vc_in, ks, kw, vs_t, vw_t = _nsa_proj(x, g, w_in_p, bsz, s)
    kc = _compress(kc_in.reshape(bsz, s, kv), pe_k, k_w1, k_b1, k_w2, transpose_out=False)
    vc_t = _compress(vc_in.reshape(bsz, s, kv), pe_v, v_w1, v_b1, v_w2, transpose_out=True)
    msel_t = _sel_matrix_t(s // CMP_STRIDE, s // SEL_BLOCK)
    attn = _nsa_attention(q, gt, b_gate, kc, vc_t, ks, vs_t, kw, vw_t, msel_t, bsz, s)
    return [(attn, w_out.astype(BF16))]


def kernel(x, ffn1_norm, ffn1_w_gu, ffn1_w_down, mix_norm, ffn2_norm, ffn2_w_gu, ffn2_w_down,
           ab_w_in, rg_conv_w, rg_conv_b, rg_w_r, rg_b_r, rg_w_i, rg_b_i, rg_lambda,
           ml_conv_w, ml_conv_b, ml_b_i, ml_b_f, ml_norm, ab_w_out,
           nsa_w_in, nsa_pe_k, nsa_k_w1, nsa_k_b1, nsa_k_w2, nsa_pe_v, nsa_v_w1, nsa_v_b1, nsa_v_w2,
           nsa_b_gate, nsa_w_out, final_norm):
    bsz, s, d = x.shape
    depth = ffn1_norm.shape[0]
    h = x.reshape(bsz * s, d)
    w1_gu, w1_down = ffn1_w_gu.astype(BF16), ffn1_w_down.astype(BF16)
    w2_gu, w2_down = ffn2_w_gu.astype(BF16), ffn2_w_down.astype(BF16)
    for i in range(depth):
        j = i // 2
        h = _ffn(h, [], ffn1_norm[i], w1_gu, w1_down, final_norm, layer=i, final=False)
        if i % 2 == 0:
            mix = _ab_mixer(h, mix_norm[i], ab_w_in[j], rg_conv_w[j], rg_conv_b[j], rg_w_r[j], rg_b_r[j],
                            rg_w_i[j], rg_b_i[j], rg_lambda[j], ml_conv_w[j], ml_conv_b[j], ml_b_i[j],
                            ml_b_f[j], ml_norm[j], ab_w_out[j], bsz, s)
        else:
            mix = _nsa_mixer(h, mix_norm[i], nsa_w_in[j], nsa_pe_k[j], nsa_k_w1[j], nsa_k_b1[j],
                             nsa_k_w2[j], nsa_pe_v[j], nsa_v_w1[j], nsa_v_b1[j], nsa_v_w2[j],
                             nsa_b_gate[j], nsa_w_out[j], bsz, s)
        h = _ffn(h, mix, ffn2_norm[i], w2_gu, w2_down, final_norm, layer=i, final=(i == depth - 1))
    return h.reshape(bsz, s, d)
```

```python
import functools

import jax
import jax.numpy as jnp
import numpy as np
from jax import lax
from jax.experimental import pallas as pl
from jax.experimental.pallas import tpu as pltpu

F32 = jnp.float32
BF16 = jnp.bfloat16
HIGHEST = lax.Precision.HIGHEST

EPS = 1e-6
RG_C = 8.0
RG_BLOCKS = 8
CONV_WIDTH = 4
ML_HEADS = 4
ML_CHUNK = 128
NSA_HEADS = 16
NSA_GROUPS = 4
NSA_HEAD_DIM = 64
NSA_BRANCHES = 3
CMP_BLOCK = 32
CMP_STRIDE = 16
SEL_BLOCK = 64
SEL_TOPN = 8
WINDOW = 256
FORCED_SCORE = 1e6

LANES = 128
SUBLANES = 8
MXU_TILE = 256
NEG = -1e30
TAKEN = -2.0
VMEM_LIMIT = 48 * 1024 * 1024

TOKEN_TILE = 512
NSA_QTILE = 256
NSA_VROWS = 80
NSA_STEP_GROUPS = 2
LOG2E = 1.4426950408889634
NT_DIMS = (((1,), (1,)), ((), ()))


def _params(n_axes):
    return pltpu.CompilerParams(dimension_semantics=("arbitrary",) * n_axes,
                                vmem_limit_bytes=VMEM_LIMIT)


def _resident(shape, index_map):
    return pl.BlockSpec(shape, index_map, pipeline_mode=pl.Buffered(1))


def _rms(x, g):
    return x * lax.rsqrt(jnp.mean(x * x, axis=-1, keepdims=True) + EPS) * g


def _gelu_tanh(x):
    return 0.5 * x * (1.0 + jnp.tanh(0.7978845608028654 * (x + 0.044715 * (x * x * x))))


def _softplus(z):
    return jnp.maximum(z, 0.0) + jnp.log1p(jnp.exp(-jnp.abs(z)))


def _sigmoid(z):
    return 0.5 * jnp.tanh(0.5 * z) + 0.5


def _sqrt_nonneg(z):
    return jnp.where(z > 0.0, z * lax.rsqrt(z), 0.0)


def _ffn_kernel(*refs, n_mix, n_chunks, final):
    x_ref = refs[0]
    a_refs = refs[1:1 + n_mix]
    wo_refs = refs[1 + n_mix:1 + 2 * n_mix]
    g_ref, wg_ref, wu_ref, wd_ref, gf_ref, o_ref = refs[1 + 2 * n_mix:]
    x = x_ref[...]
    for a_ref, wo_ref in zip(a_refs, wo_refs):
        x = x + jnp.dot(a_ref[...], wo_ref[...], preferred_element_type=F32)
    xn = _rms(x, g_ref[...]).astype(BF16)
    n_tiles = wg_ref.shape[1] // MXU_TILE
    bounds = [MXU_TILE * ((n_tiles * c + n_chunks - 1) // n_chunks) for c in range(n_chunks + 1)]
    acc = None
    for lo, hi in zip(bounds[:-1], bounds[1:]):
        gate = jnp.dot(xn, wg_ref[:, lo:hi], preferred_element_type=F32)
        up = jnp.dot(xn, wu_ref[:, lo:hi], preferred_element_type=F32)
        h = (gate * jax.nn.sigmoid(gate) * up).astype(BF16)
        part = jnp.dot(h, wd_ref[lo:hi, :], preferred_element_type=F32)
        acc = part if acc is None else acc + part
    y = x + 0.5 * acc
    if final:
        y = _rms(y, gf_ref[...])
    o_ref[...] = y


def _ffn(x, mix, g, w_gu, w_down, gf, *, layer, final):
    n, d = x.shape
    f = w_down.shape[1]
    tm = min(TOKEN_TILE, n)
    const = lambda i: (0, 0)
    row = lambda i: (i, 0)
    acts = [a for a, _ in mix]
    wos = [w for _, w in mix]
    return pl.pallas_call(
        functools.partial(_ffn_kernel, n_mix=len(mix), n_chunks=2, final=final),
        grid=(n // tm,),
        in_specs=[pl.BlockSpec((tm, d), row)]
        + [pl.BlockSpec((tm, a.shape[1]), row) for a in acts]
        + [_resident(w.shape, const) for w in wos]
        + [pl.BlockSpec((1, d), const),
           _resident((None, d, f), lambda i: (layer, 0, 0)),
           _resident((None, d, f), lambda i: (layer, 0, 1)),
           _resident((None, f, d), lambda i: (layer, 0, 0)),
           pl.BlockSpec((1, d), const)],
        out_specs=pl.BlockSpec((tm, d), row),
        out_shape=jax.ShapeDtypeStruct((n, d), F32),
        compiler_params=_params(1),
        name="ffn_final" if final else ("ffn_mix" if mix else "ffn"),
    )(x, *acts, *wos, g.reshape(1, d), w_gu, w_gu, w_down, gf.reshape(1, d))


def _norm_matmul_kernel(x_ref, g_ref, w_ref, o_ref):
    xn = _rms(x_ref[...], g_ref[...]).astype(BF16)
    o_ref[...] = jnp.dot(xn, w_ref[...], preferred_element_type=F32)


def _norm_matmul(x, g, w):
    n, d = x.shape
    dout = w.shape[1]
    tm = min(TOKEN_TILE, n)
    const = lambda i: (0, 0)
    return pl.pallas_call(
        _norm_matmul_kernel,
        grid=(n // tm,),
        in_specs=[pl.BlockSpec((tm, d), lambda i: (i, 0)),
                  pl.BlockSpec((1, d), const),
                  _resident((d, dout), const)],
        out_specs=pl.BlockSpec((tm, dout), lambda i: (i, 0)),
        out_shape=jax.ShapeDtypeStruct((n, dout), F32),
        compiler_params=_params(1),
        name="norm_matmul",
    )(x, g.reshape(1, d), w)


def _causal_conv(x, cw, cb, pad_ref):
    s = x.shape[0]
    pad_ref[0:SUBLANES, :] = jnp.zeros((SUBLANES, x.shape[1]), F32)
    pad_ref[SUBLANES:SUBLANES + s, :] = x
    y = cb + cw[CONV_WIDTH - 1:CONV_WIDTH, :] * x
    for j in range(CONV_WIDTH - 1):
        off = SUBLANES - (CONV_WIDTH - 1 - j)
        y = y + cw[j:j + 1, :] * pad_ref[off:off + s, :]
    return y


def _causal_conv_blocks(x, cw, cb, pad_ref, emit, block=256):
    s = x.shape[0]
    pad_ref[0:SUBLANES, :] = jnp.zeros((SUBLANES, x.shape[1]), F32)
    pad_ref[SUBLANES:SUBLANES + s, :] = x
    for r0 in range(0, s, block):
        y = cb + cw[CONV_WIDTH - 1:CONV_WIDTH, :] * pad_ref[SUBLANES + r0:SUBLANES + r0 + block, :]
        for j in range(CONV_WIDTH - 1):
            off = r0 + SUBLANES - (CONV_WIDTH - 1 - j)
            y = y + cw[j:j + 1, :] * pad_ref[off:off + block, :]
        emit(r0, y)


def _rglru_kernel(xa_ref, ga_ref, cw_ref, cb_ref, wr_ref, br_ref, wi_ref, bi_ref, lam_ref,
                  o_ref, pad_ref, a_ref, u_ref):
    s = xa_ref.shape[1]
    ng = s // SUBLANES
    xc = _causal_conv(xa_ref[0], cw_ref[...], cb_ref[...], pad_ref)
    xb = xc.astype(BF16)
    r = _sigmoid(jnp.dot(xb, wr_ref[...], preferred_element_type=F32) + br_ref[...])
    ig = _sigmoid(jnp.dot(xb, wi_ref[...], preferred_element_type=F32) + bi_ref[...])
    log_a = -RG_C * r * _softplus(-lam_ref[...])
    a = jnp.exp(log_a)
    th = jnp.tanh(log_a)
    u = _sqrt_nonneg(-2.0 * th / (1.0 - th)) * (ig * xc)
    a3 = a.reshape(ng, SUBLANES, LANES)
    u3 = u.reshape(ng, SUBLANES, LANES)
    row = lax.broadcasted_iota(jnp.int32, (ng, SUBLANES, LANES), 1)
    sh = 1
    while sh < SUBLANES:
        a_s = pltpu.roll(a3, sh, 1)
        u_s = pltpu.roll(u3, sh, 1)
        m = row >= sh
        u3 = jnp.where(m, a3 * u_s + u3, u3)
        a3 = jnp.where(m, a3 * a_s, a3)
        sh *= 2
    a_ref[...] = a3
    u_ref[...] = u3

    def body(i, h):
        hh = a_ref[i] * h + u_ref[i]
        u_ref[i] = hh
        return hh[SUBLANES - 1:SUBLANES, :]

    lax.fori_loop(0, ng, body, jnp.zeros((1, LANES), F32), unroll=True)
    o_ref[0] = (_gelu_tanh(ga_ref[0]) * u_ref[...].reshape(s, LANES)).astype(o_ref.dtype)


def _rglru(proj3, cw, cb, wr_bd, br, wi_bd, bi, lam):
    bsz, s, _ = proj3.shape
    c = cw.shape[1]
    nb = c // LANES
    vec = lambda b, j: (0, j)
    return pl.pallas_call(
        _rglru_kernel,
        grid=(bsz, nb),
        in_specs=[pl.BlockSpec((1, s, LANES), lambda b, j: (b, 0, j)),
                  pl.BlockSpec((1, s, LANES), lambda b, j: (b, 0, nb + j)),
                  pl.BlockSpec((CONV_WIDTH, LANES), vec),
                  pl.BlockSpec((1, LANES), vec),
                  pl.BlockSpec((LANES, LANES), lambda b, j: (j, j)),
                  pl.BlockSpec((1, LANES), vec),
                  pl.BlockSpec((LANES, LANES), lambda b, j: (j, j)),
                  pl.BlockSpec((1, LANES), vec),
                  pl.BlockSpec((1, LANES), vec)],
        out_specs=pl.BlockSpec((1, s, LANES), lambda b, j: (b, 0, j)),
        out_shape=jax.ShapeDtypeStruct((bsz, s, c), BF16),
        scratch_shapes=[pltpu.VMEM((s + SUBLANES, LANES), F32),
                        pltpu.VMEM((s // SUBLANES, SUBLANES, LANES), F32),
                        pltpu.VMEM((s // SUBLANES, SUBLANES, LANES), F32)],
        compiler_params=_params(2),
        name="rglru",
    )(proj3, proj3, cw, cb.reshape(1, c), wr_bd, br.reshape(1, c), wi_bd, bi.reshape(1, c),
      lam.reshape(1, c))


def _mlstm_kernel(bias_ref, q_ref, k_ref, v_ref, og_ref, gi_ref, gf_ref, cwq_ref, cbq_ref,
                  cwk_ref, cbk_ref, ng_ref, o_ref, pad_ref, q_sc, k_sc, li_sc, lf_sc, bc_sc):
    hd = pl.program_id(1)
    s = q_ref.shape[1]
    dh = q_ref.shape[2]
    nc = s // ML_CHUNK

    def put_q(r0, y):
        q_sc[r0:r0 + y.shape[0], :] = jax.nn.silu(y) * (dh ** -0.5)

    def put_k(r0, y):
        k_sc[r0:r0 + y.shape[0], :] = jax.nn.silu(y)

    _causal_conv_blocks(q_ref[0], cwq_ref[...], cbq_ref[...], pad_ref, put_q)
    _causal_conv_blocks(k_ref[0], cwk_ref[...], cbk_ref[...], pad_ref, put_k)
    li_sc[...] = gi_ref[0, 0] + bias_ref[hd]
    lf = -_softplus(-(gf_ref[0, 0] + bias_ref[ML_HEADS + hd]))
    lf_sc[...] = lf
    jj = lax.broadcasted_iota(jnp.int32, (ML_CHUNK, ML_CHUNK), 0)
    kk = lax.broadcasted_iota(jnp.int32, (ML_CHUNK, ML_CHUNK), 1)
    tri = kk <= jj
    upper = jnp.where(jj <= kk, 1.0, 0.0).astype(F32)
    bc_sc[...] = jnp.dot(lf, upper, precision=HIGHEST, preferred_element_type=F32)
    norm_g = ng_ref[...]

    def chunk(c, carry):
        c_st, n_st, m_st = carry
        r0 = pl.multiple_of(c * ML_CHUNK, ML_CHUNK)
        qc = q_sc[pl.ds(r0, ML_CHUNK), :]
        kc = k_sc[pl.ds(r0, ML_CHUNK), :]
        vc = v_ref[0, pl.ds(r0, ML_CHUNK), :]
        li_row = li_sc[pl.ds(c, 1), :]
        lf_row = lf_sc[pl.ds(c, 1), :]
        b_row = bc_sc[pl.ds(c, 1), :]
        b_col = jnp.sum(jnp.where(tri, lf_row, 0.0), axis=1, keepdims=True)
        g_tot = jnp.sum(lf_row, axis=1, keepdims=True)
        w_row = g_tot - b_row + li_row
        m_loc = jnp.max(w_row, axis=1, keepdims=True)
        wk = jnp.exp(w_row - m_loc)
        lhs = jnp.concatenate([vc.T * wk, jnp.broadcast_to(wk, (2 * SUBLANES, ML_CHUNK))], axis=0)
        cn = jnp.dot(lhs.astype(BF16), kc.astype(BF16), preferred_element_type=F32)
        c_loc = cn[0:dh]
        n_loc = cn[dh:dh + 1]
        d = jnp.where(tri, b_col - b_row + li_row, NEG)
        m_inter = b_col + m_st
        m = jnp.maximum(m_inter, jnp.max(d, axis=1, keepdims=True))
        qb = qc.astype(BF16)
        qk = lax.dot_general(qb, kc.astype(BF16), NT_DIMS, preferred_element_type=F32)
        p = jnp.exp(d - m) * qk
        sc = jnp.exp(m_inter - m)
        inter = lax.dot_general(qb, c_st.astype(BF16), NT_DIMS, preferred_element_type=F32)
        num = sc * inter + jnp.dot(p.astype(BF16), vc.astype(BF16), preferred_element_type=F32)
        den = jnp.sum(sc * (qc * n_st) + p, axis=1, keepdims=True)
        h = num / jnp.maximum(jnp.abs(den), jnp.exp(-m))
        h = h * lax.rsqrt(jnp.mean(h * h, axis=1, keepdims=True) + EPS) * norm_g
        o_ref[0, pl.ds(r0, ML_CHUNK), :] = (
            jax.nn.sigmoid(og_ref[0, pl.ds(r0, ML_CHUNK), :]) * h).astype(o_ref.dtype)
        m_new = jnp.maximum(g_tot + m_st, m_loc)
        sa = jnp.exp(g_tot + m_st - m_new)
        sb = jnp.exp(m_loc - m_new)
        return sa * c_st + sb * c_loc, sa * n_st + sb * n_loc, m_new

    init = (jnp.zeros((dh, dh), F32), jnp.zeros((1, dh), F32), jnp.full((1, 1), NEG, F32))
    lax.fori_loop(0, nc, chunk, init, unroll=True)


def _mlstm(proj3, gates_t, bias, cw, cb, norm_g, *, col0):
    bsz, s, _ = proj3.shape
    nh = ML_HEADS
    dh = LANES
    nc = s // ML_CHUNK
    base = col0 // dh
    blk = lambda off: pl.BlockSpec((1, s, dh), lambda b, h: (b, 0, base + off + h))
    vec = lambda off: (lambda b, h: (0, off + h))
    return pl.pallas_call(
        _mlstm_kernel,
        grid=(bsz, nh),
        in_specs=[pl.BlockSpec(memory_space=pltpu.SMEM),
                  blk(0), blk(nh), blk(2 * nh), blk(3 * nh),
                  pl.BlockSpec((1, 1, nc, ML_CHUNK), lambda b, h: (b, h, 0, 0)),
                  pl.BlockSpec((1, 1, nc, ML_CHUNK), lambda b, h: (b, nh + h, 0, 0)),
                  pl.BlockSpec((CONV_WIDTH, dh), vec(0)), pl.BlockSpec((1, dh), vec(0)),
                  pl.BlockSpec((CONV_WIDTH, dh), vec(nh)), pl.BlockSpec((1, dh), vec(nh)),
                  pl.BlockSpec((1, dh), vec(0))],
        out_specs=pl.BlockSpec((1, s, dh), lambda b, h: (b, 0, h)),
        out_shape=jax.ShapeDtypeStruct((bsz, s, nh * dh), BF16),
        scratch_shapes=[pltpu.VMEM((s + SUBLANES, dh), F32),
                        pltpu.VMEM((s, dh), F32), pltpu.VMEM((s, dh), F32),
                        pltpu.VMEM((nc, ML_CHUNK), F32), pltpu.VMEM((nc, ML_CHUNK), F32),
                        pltpu.VMEM((nc, ML_CHUNK), F32)],
        compiler_params=_params(2),
        name="mlstm",
    )(bias, proj3, proj3, proj3, proj3, gates_t, gates_t, cw, cb.reshape(1, -1), cw,
      cb.reshape(1, -1), norm_g.reshape(1, -1))


def _ab_mixer(x, g, w_in, rg_conv_w, rg_conv_b, rg_w_r, rg_b_r, rg_w_i, rg_b_i, rg_lambda,
              ml_conv_w, ml_conv_b, ml_b_i, ml_b_f, ml_norm, w_out, bsz, s):
    d_rg = rg_conv_w.shape[1]
    d_ml = ml_norm.shape[0]
    d_main = 2 * d_rg + 4 * d_ml
    n_gate = 2 * ML_HEADS
    w_in_p = jnp.pad(w_in, ((0, 0), (0, LANES - n_gate))).astype(BF16)
    proj = _norm_matmul(x, g, w_in_p)
    proj3 = proj.reshape(bsz, s, proj.shape[1])
    bd = lambda w: jax.scipy.linalg.block_diag(*[w[i] for i in range(RG_BLOCKS)]).astype(BF16)
    ya = _rglru(proj3, rg_conv_w, rg_conv_b, bd(rg_w_r), rg_b_r, bd(rg_w_i), rg_b_i, rg_lambda)
    gates_t = proj3[:, :, d_main:d_main + n_gate].transpose(0, 2, 1).reshape(
        bsz, n_gate, s // ML_CHUNK, ML_CHUNK)
    yb = _mlstm(proj3, gates_t, jnp.concatenate([ml_b_i, ml_b_f]), ml_conv_w, ml_conv_b, ml_norm,
                col0=2 * d_rg)
    w_out_b = w_out.astype(BF16)
    return [(ya.reshape(bsz * s, d_rg), w_out_b[:d_rg]), (yb.reshape(bsz * s, d_ml), w_out_b[d_rg:])]


def _nsa_proj_kernel(x_ref, g_ref, w_ref, q_ref, gt_ref, kc_ref, vc_ref, ks_ref, kw_ref, vs_ref, vw_ref,
                     *, seq):
    tm = x_ref.shape[0]
    kv = NSA_GROUPS * NSA_HEAD_DIM
    width = NSA_HEADS * NSA_HEAD_DIM
    half = LANES // 2
    xn = _rms(x_ref[...], g_ref[...]).astype(BF16)
    proj = jnp.dot(xn, w_ref[...], preferred_element_type=F32)
    lane = lax.broadcasted_iota(jnp.int32, (tm, LANES), 1)
    lo = lane < half

    def lane_pair(c0):
        p = proj[:, c0:c0 + LANES]
        return p, pltpu.roll(p, half, 1)

    scale = NSA_HEAD_DIM ** -0.5 * LOG2E
    for pair in range(NSA_HEADS // 2):
        for j, piece in enumerate(lane_pair(pair * LANES)):
            h = 2 * pair + j
            q_ref[:, h * LANES:(h + 1) * LANES] = jnp.where(lo, piece * scale, 0.0).astype(BF16)
    gt_ref[0] = proj[:, width + 6 * kv:width + 6 * kv + LANES].T
    kc_ref[...] = proj[:, width:width + kv]
    vc_ref[...] = proj[:, width + kv:width + 2 * kv]
    pos = (pl.program_id(0) % (seq // tm)) * tm + lax.broadcasted_iota(jnp.int32, (tm, LANES), 0)
    onehot = jnp.where(lane == half + pos // SEL_BLOCK, 1.0, 0.0)
    for pair in range(NSA_GROUPS // 2):
        for idx, ref, fill in ((2, ks_ref, onehot), (4, kw_ref, 0.0)):
            for j, piece in enumerate(lane_pair(width + idx * kv + pair * LANES)):
                ref[0, 2 * pair + j] = jnp.where(lo, piece, fill).astype(BF16)
        for idx, ref in ((3, vs_ref), (5, vw_ref)):
            c0 = width + idx * kv + pair * LANES
            t = proj[:, c0:c0 + LANES].T.astype(BF16)
            ones = jnp.ones((NSA_VROWS - half, NSA_QTILE), BF16)
            for j in range(tm // NSA_QTILE):
                for hf in range(2):
                    ref[0, 2 * pair + hf, j, 0:half, :] = t[hf * half:(hf + 1) * half,
                                                            j * NSA_QTILE:(j + 1) * NSA_QTILE]
                    ref[0, 2 * pair + hf, j, half:NSA_VROWS, :] = ones


def _nsa_proj(x, g, w, bsz, s):
    n, d = x.shape
    dout = w.shape[1]
    tm = min(TOKEN_TILE, s)
    nsb = s // tm
    ng, dk = NSA_GROUPS, NSA_HEAD_DIM
    kv = ng * dk
    tq = NSA_QTILE
    const = lambda i: (0, 0)
    row = lambda i: (i, 0)
    keyed = lambda i: (i // nsb, 0, i % nsb, 0)
    return pl.pallas_call(
        functools.partial(_nsa_proj_kernel, seq=s),
        grid=(n // tm,),
        in_specs=[pl.BlockSpec((tm, d), row), pl.BlockSpec((1, d), const), _resident((d, dout), const)],
        out_specs=[pl.BlockSpec((tm, NSA_HEADS * LANES), row),
                   pl.BlockSpec((1, LANES, tm), lambda i: (i // nsb, 0, i % nsb)),
                   pl.BlockSpec((tm, kv), row), pl.BlockSpec((tm, kv), row),
                   pl.BlockSpec((1, ng, tm, LANES), keyed), pl.BlockSpec((1, ng, tm, LANES), keyed),
                   pl.BlockSpec((1, ng, tm // tq, NSA_VROWS, tq), lambda i: (i // nsb, 0, i % nsb, 0, 0)),
                   pl.BlockSpec((1, ng, tm // tq, NSA_VROWS, tq), lambda i: (i // nsb, 0, i % nsb, 0, 0))],
        out_shape=[jax.ShapeDtypeStruct((n, NSA_HEADS * LANES), BF16),
                   jax.ShapeDtypeStruct((bsz, LANES, s), F32),
                   jax.ShapeDtypeStruct((n, kv), F32), jax.ShapeDtypeStruct((n, kv), F32),
                   jax.ShapeDtypeStruct((bsz, ng, s, LANES), BF16),
                   jax.ShapeDtypeStruct((bsz, ng, s, LANES), BF16),
                   jax.ShapeDtypeStruct((bsz, ng, s // tq, NSA_VROWS, tq), BF16),
                   jax.ShapeDtypeStruct((bsz, ng, s // tq, NSA_VROWS, tq), BF16)],
        compiler_params=_params(1),
        name="nsa_proj",
    )(x, g.reshape(1, d), w)


def _compress_kernel(x_ref, pe_ref, w1_ref, b1_ref, w2_ref, o_ref, *, transpose_out):
    nb = x_ref.shape[1] // CMP_STRIDE
    top = bot = None
    for l in range(CMP_STRIDE):
        rows = x_ref[0, pl.ds(l, nb, stride=CMP_STRIDE), :]
        t = jnp.dot((rows + pe_ref[l:l + 1, :]).astype(BF16), w1_ref[l], preferred_element_type=F32)
        b = jnp.dot((rows + pe_ref[CMP_STRIDE + l:CMP_STRIDE + l + 1, :]).astype(BF16),
                    w1_ref[CMP_STRIDE + l], preferred_element_type=F32)
        top = t if top is None else top + t
        bot = b if bot is None else bot + b
    pre = top + pltpu.roll(bot, nb - 1, 0) + b1_ref[...]
    out = jnp.dot(_gelu_tanh(pre).astype(BF16), w2_ref[...], preferred_element_type=F32)
    out = jnp.where(lax.broadcasted_iota(jnp.int32, out.shape, 0) < nb - 1, out, 0.0)
    for hf in range(2):
        piece = out[:, hf * LANES:(hf + 1) * LANES]
        if transpose_out:
            o_ref[0, hf] = piece.T[0:NSA_HEAD_DIM, :].astype(o_ref.dtype)
        else:
            o_ref[0, hf] = piece.astype(o_ref.dtype)


def _compress(kv_in, pe, w1, b1, w2, *, transpose_out):
    bsz, s, _ = kv_in.shape
    dk, hid = NSA_HEAD_DIM, w1.shape[1]
    nb = s // CMP_STRIDE
    w1r = w1.reshape(CMP_BLOCK, dk, hid)
    z = jnp.zeros_like(w1r)
    w1p = jnp.concatenate([jnp.concatenate([w1r, z], axis=2), jnp.concatenate([z, w1r], axis=2)],
                          axis=1).astype(BF16)
    w2w = jnp.pad(w2, ((0, 0), (0, LANES - dk)))
    z2 = jnp.zeros_like(w2w)
    w2p = jnp.concatenate([jnp.concatenate([w2w, z2], axis=1), jnp.concatenate([z2, w2w], axis=1)],
                          axis=0).astype(BF16)
    const2 = lambda b, p: (0, 0)
    out_block = (1, 2, dk, nb) if transpose_out else (1, 2, nb, LANES)
    out_shape = (bsz, NSA_GROUPS, dk, nb) if transpose_out else (bsz, NSA_GROUPS, nb, LANES)
    return pl.pallas_call(
        functools.partial(_compress_kernel, transpose_out=transpose_out),
        grid=(bsz, NSA_GROUPS // 2),
        in_specs=[pl.BlockSpec((1, s, LANES), lambda b, p: (b, 0, p)),
                  pl.BlockSpec((CMP_BLOCK, LANES), const2),
                  _resident((CMP_BLOCK, LANES, 2 * hid), lambda b, p: (0, 0, 0)),
                  pl.BlockSpec((1, 2 * hid), const2),
                  _resident((2 * hid, 2 * LANES), const2)],
        out_specs=pl.BlockSpec(out_block, lambda b, p: (b, p, 0, 0)),
        out_shape=jax.ShapeDtypeStruct(out_shape, BF16),
        compiler_params=_params(2),
        name="nsa_compress",
    )(kv_in, jnp.concatenate([pe, pe], axis=1), w1p, jnp.concatenate([b1, b1]).reshape(1, 2 * hid), w2p)


def _nsa_kernel(*refs):
    step = pl.program_id(2)
    q_ref, ks_ref = refs[1], refs[5]
    for qi in range(ks_ref.shape[2] // q_ref.shape[0]):
        pl.when(step == qi)(functools.partial(_nsa_step, qi, *refs))


def _nsa_step(qi, bg_ref, q_ref, gt_ref, kc_ref, vc_ref, ks_ref, vs_ref, kw_ref, vw_ref, msel_ref,
              o_ref, qa_sc, s_sc, bias_sc, m_sc, acc_sc, part_sc):
    gp = pl.program_id(1)
    tq = q_ref.shape[0]
    rep = NSA_HEADS // NSA_GROUPS
    dk = NSA_HEAD_DIM
    half = LANES // 2
    n_sel = 32
    units = range(NSA_STEP_GROUPS)
    heads = range(rep)
    s0 = qi * tq
    key_off = lax.broadcasted_iota(jnp.int32, (tq, tq), 0)
    qry_off = lax.broadcasted_iota(jnp.int32, (tq, tq), 1)
    cols = lambda r: slice(r * tq, (r + 1) * tq)
    nt_dot = lambda k, q: lax.dot_general(k, q, NT_DIMS, preferred_element_type=F32)
    q_head = lambda u, r: q_ref[:, (u * rep + r) * LANES:(u * rep + r + 1) * LANES]

    def gate_row(u, r, br):
        idx = (gp * NSA_STEP_GROUPS + u) * (rep * NSA_BRANCHES) + r * NSA_BRANCHES + br
        return jax.nn.sigmoid(gt_ref[0, pl.ds(idx, 1), :] + bg_ref[idx])

    has_prev = qi >= 1
    kp = max(qi - 1, 0)
    p0 = kp * tq
    d0 = qi * tq
    cmp_scores = [[nt_dot(kc_ref[0, u], q_head(u, r)) for r in heads] for u in units]
    win_scores = [[(nt_dot(kw_ref[0, u, pl.ds(p0, tq), :], q_head(u, r)) if has_prev else None,
                    nt_dot(kw_ref[0, u, pl.ds(d0, tq), :], q_head(u, r))) for r in heads] for u in units]
    for u in units:
        for r in heads:
            s_sc[1, u, r] = nt_dot(ks_ref[0, u, pl.ds(d0, tq), :], q_head(u, r))

    n_blk = kc_ref.shape[2]
    blk_id = lax.broadcasted_iota(jnp.int32, (n_blk, tq), 0)
    t_blk = s0 + lax.broadcasted_iota(jnp.int32, (n_blk, tq), 1)
    cmask = (blk_id * CMP_STRIDE + (CMP_BLOCK - 1) <= t_blk) & (blk_id < n_blk - 1)
    psum = [None] * NSA_STEP_GROUPS
    for r in heads:
        for u in units:
            scm = jnp.where(cmask, cmp_scores[u][r], NEG)
            e = jnp.where(cmask, jnp.exp2(scm - jnp.max(scm, axis=0, keepdims=True)), 0.0)
            den = jnp.sum(e, axis=0, keepdims=True)
            p = e * (1.0 / jnp.where(den > 0.0, den, 1.0))
            psum[u] = p if psum[u] is None else psum[u] + p
            part_sc[u, :, cols(r)] = gate_row(u, r, 0) * jnp.dot(vc_ref[0, u], p.astype(BF16),
                                                                 preferred_element_type=F32)

    jb = lax.broadcasted_iota(jnp.int32, (n_sel, tq), 0)
    jbf = jb.astype(F32)
    cur = (s0 + lax.broadcasted_iota(jnp.int32, (n_sel, tq), 1)) // SEL_BLOCK
    valid = jb <= cur
    forced = ((jb == 0) | (jb == cur) | (jb == cur - 1)) & valid
    for u in units:
        p_hi = psum[u].astype(BF16)
        p_lo = (psum[u] - p_hi.astype(F32)).astype(BF16)
        score_t = (jnp.dot(msel_ref[...], p_hi, preferred_element_type=F32)
                   + jnp.dot(msel_ref[...], p_lo, preferred_element_type=F32))
        s2 = jnp.where(forced, FORCED_SCORE, jnp.where(valid, score_t[half:half + n_sel, :], -1.0))
        for _ in range(SEL_TOPN):
            best = jnp.max(s2, axis=0, keepdims=True)
            first = jnp.min(jnp.where(s2 == best, jbf, float(n_sel)), axis=0, keepdims=True)
            s2 = jnp.where(jbf == first, TAKEN, s2)
        bias_t = jnp.where((s2 == TAKEN) & valid, 0.0, NEG)
        bias_sc[u] = bias_t
        bias_full = jnp.concatenate(
            [jnp.zeros((half, tq), F32), bias_t, jnp.zeros((LANES - half - n_sel, tq), F32)], axis=0)
        bias_q = bias_full.T
        for r in heads:
            qa_sc[u, r] = (q_head(u, r).astype(F32) + bias_q).astype(BF16)

    prev_mask = key_off > qry_off
    diag_mask = key_off <= qry_off
    for r in heads:
        for u in units:
            s_diag = jnp.where(diag_mask, win_scores[u][r][1], NEG)
            mw = jnp.max(s_diag, axis=0, keepdims=True)
            if has_prev:
                s_prev = jnp.where(prev_mask, win_scores[u][r][0], NEG)
                mw = jnp.maximum(jnp.max(s_prev, axis=0, keepdims=True), mw)
            win = jnp.dot(vw_ref[0, u, qi], jnp.exp2(s_diag - mw).astype(BF16),
                          preferred_element_type=F32)
            if has_prev:
                win = win + jnp.dot(vw_ref[0, u, kp], jnp.exp2(s_prev - mw).astype(BF16),
                                    preferred_element_type=F32)
            part_sc[u, :, cols(r)] += gate_row(u, r, 2) * (win[0:dk] * (1.0 / win[dk:dk + 1]))

    def qk_into(kt, slot):
        for u in units:
            k_tile = ks_ref[0, u, kt * tq:(kt + 1) * tq, :]
            for r in heads:
                s_sc[slot, u, r] = nt_dot(k_tile, qa_sc[u, r])

    if qi >= 1:
        qk_into(0, 0)
    blk0 = qi * (tq // SEL_BLOCK)
    for u in units:
        bias_diag = jnp.concatenate(
            [jnp.broadcast_to(bias_sc[u, pl.ds(blk0 + jj, 1), :], (SEL_BLOCK, tq))
             for jj in range(tq // SEL_BLOCK)], axis=0)
        for r in heads:
            s = jnp.where(diag_mask, s_sc[1, u, r] + bias_diag, NEG)
            m0 = jnp.max(s, axis=0, keepdims=True)
            m_sc[u, :, cols(r)] = m0
            acc_sc[u, :, cols(r)] = jnp.dot(vs_ref[0, u, qi], jnp.exp2(s - m0).astype(BF16),
                                            preferred_element_type=F32)

    def softmax_pv(kt, slot):
        for u in units:
            v_t = vs_ref[0, u, kt]
            for r in heads:
                s = s_sc[slot, u, r]
                m_old = m_sc[u, :, cols(r)]
                m_new = jnp.maximum(m_old, jnp.max(s, axis=0, keepdims=True))
                alpha = jnp.exp2(m_old - m_new)
                p = jnp.exp2(s - m_new)
                acc_sc[u, :, cols(r)] = alpha * acc_sc[u, :, cols(r)] + jnp.dot(
                    v_t, p.astype(BF16), preferred_element_type=F32)
                m_sc[u, :, cols(r)] = m_new

    for kt in range(qi):
        if kt + 1 < qi:
            qk_into(kt + 1, (kt + 1) % 2)
        softmax_pv(kt, kt % 2)

    for u in units:
        merged = []
        for r in heads:
            o_slc = acc_sc[u, 0:dk, cols(r)] * (1.0 / acc_sc[u, dk:dk + 1, cols(r)])
            merged.append(part_sc[u, :, cols(r)] + gate_row(u, r, 1) * o_slc)
        for pair in range(rep // 2):
            both = jnp.concatenate([merged[2 * pair], merged[2 * pair + 1]], axis=0)
            c0 = (u * (rep // 2) + pair) * LANES
            o_ref[:, c0:c0 + LANES] = both.T.astype(o_ref.dtype)


def _nsa_attention(q, gt, b_gate, kc, vc_t, ks, vs_t, kw, vw_t, msel_t, bsz, s):
    n = q.shape[0]
    tq = NSA_QTILE
    assert tq == WINDOW and s % tq == 0 and s // SEL_BLOCK == 32
    rep = NSA_HEADS // NSA_GROUPS
    dk = NSA_HEAD_DIM
    nq = s // tq
    ug = NSA_STEP_GROUPS
    per_bg = lambda a: pl.BlockSpec((1, ug) + a.shape[2:], lambda b, g, i: (b, g) + (0,) * (a.ndim - 2))
    return pl.pallas_call(
        _nsa_kernel,
        grid=(bsz, NSA_GROUPS // ug, nq),
        in_specs=[pl.BlockSpec(memory_space=pltpu.SMEM),
                  pl.BlockSpec((tq, ug * rep * LANES), lambda b, g, i: (b * nq + i, g)),
                  pl.BlockSpec((1, LANES, tq), lambda b, g, i: (b, 0, i)),
                  per_bg(kc), per_bg(vc_t), per_bg(ks), per_bg(vs_t), per_bg(kw), per_bg(vw_t),
                  pl.BlockSpec(msel_t.shape, lambda b, g, i: (0, 0))],
        out_specs=pl.BlockSpec((tq, ug * rep * dk), lambda b, g, i: (b * nq + i, g)),
        out_shape=jax.ShapeDtypeStruct((n, NSA_HEADS * dk), BF16),
        scratch_shapes=[pltpu.VMEM((ug, rep, tq, LANES), BF16),
                        pltpu.VMEM((2, ug, rep, tq, tq), F32),
                        pltpu.VMEM((ug, s // SEL_BLOCK, tq), F32),
                        pltpu.VMEM((ug, 1, rep * tq), F32),
                        pltpu.VMEM((ug, NSA_VROWS, rep * tq), F32),
                        pltpu.VMEM((ug, dk, rep * tq), F32)],
        compiler_params=_params(3),
        name="nsa_attention",
    )(b_gate, q, gt, kc, vc_t, ks, vs_t, kw, vw_t, msel_t)


def _sel_matrix_t(n_rows, n_sel):
    n_cmp = np.arange(n_rows)[None, :] * CMP_STRIDE
    sel = np.arange(n_sel)[:, None] * SEL_BLOCK
    ov = np.clip(np.minimum(n_cmp + CMP_BLOCK, sel + SEL_BLOCK) - np.maximum(n_cmp, sel), 0, None)
    out = np.zeros((LANES, n_rows), np.float32)
    out[LANES // 2:LANES // 2 + n_sel] = ov / CMP_BLOCK
    return jnp.asarray(out, dtype=BF16)


def _nsa_mixer(x, g, w_in, pe_k, k_w1, k_b1, k_w2, pe_v, v_w1, v_b1, v_w2, b_gate, w_out, bsz, s):
    n_gate = NSA_BRANCHES * NSA_HEADS
    kv = NSA_GROUPS * NSA_HEAD_DIM
    w_in_p = jnp.pad(w_in, ((0, 0), (0, LANES - n_gate))).astype(BF16)
    q, gt, kc_in, vc_in, ks, kw, vs_t, vw_t = _nsa_proj(x, g, w_in_p, bsz, s)
    kc = _compress(kc_in.reshape(bsz, s, kv), pe_k, k_w1, k_b1, k_w2, transpose_out=False)
    vc_t = _compress(vc_in.reshape(bsz, s, kv), pe_v, v_w1, v_b1, v_w2, transpose_out=True)
    msel_t = _sel_matrix_t(s // CMP_STRIDE, s // SEL_BLOCK)
    attn = _nsa_attention(q, gt, b_gate, kc, vc_t, ks, vs_t, kw, vw_t, msel_t, bsz, s)
    return [(attn, w_out.astype(BF16))]


def kernel(x, ffn1_norm, ffn1_w_gu, ffn1_w_down, mix_norm, ffn2_norm, ffn2_w_gu, ffn2_w_down,
           ab_w_in, rg_conv_w, rg_conv_b, rg_w_r, rg_b_r, rg_w_i, rg_b_i, rg_lambda,
           ml_conv_w, ml_conv_b, ml_b_i, ml_b_f, ml_norm, ab_w_out,
           nsa_w_in, nsa_pe_k, nsa_k_w1, nsa_k_b1, nsa_k_w2, nsa_pe_v, nsa_v_w1, nsa_v_b1, nsa_v_w2,
           nsa_b_gate, nsa_w_out, final_norm):
    bsz, s, d = x.shape
    depth = ffn1_norm.shape[0]
    h = x.reshape(bsz * s, d)
    w1_gu, w1_down = ffn1_w_gu.astype(BF16), ffn1_w_down.astype(BF16)
    w2_gu, w2_down = ffn2_w_gu.astype(BF16), ffn2_w_down.astype(BF16)
    for i in range(depth):
        j = i // 2
        h = _ffn(h, [], ffn1_norm[i], w1_gu, w1_down, final_norm, layer=i, final=False)
        if i % 2 == 0:
            mix = _ab_mixer(h, mix_norm[i], ab_w_in[j], rg_conv_w[j], rg_conv_b[j], rg_w_r[j], rg_b_r[j],
                            rg_w_i[j], rg_b_i[j], rg_lambda[j], ml_conv_w[j], ml_conv_b[j], ml_b_i[j],
                            ml_b_f[j], ml_norm[j], ab_w_out[j], bsz, s)
        else:
            mix = _nsa_mixer(h, mix_norm[i], nsa_w_in[j], nsa_pe_k[j], nsa_k_w1[j], nsa_k_b1[j],
                             nsa_k_w2[j], nsa_pe_v[j], nsa_v_w1[j], nsa_v_b1[j], nsa_v_w2[j],
                             nsa_b_gate[j], nsa_w_out[j], bsz, s)
        h = _ffn(h, mix, ffn2_norm[i], w2_gu, w2_down, final_norm, layer=i, final=(i == depth - 1))
    return h.reshape(bsz, s, d)
```

```python
import functools

import jax
import jax.numpy as jnp
import numpy as np
from jax import lax
from jax.experimental import pallas as pl
from jax.experimental.pallas import tpu as pltpu

F32 = jnp.float32
BF16 = jnp.bfloat16
HIGHEST = lax.Precision.HIGHEST

EPS = 1e-6
RG_C = 8.0
RG_BLOCKS = 8
CONV_WIDTH = 4
ML_HEADS = 4
ML_CHUNK = 128
NSA_HEADS = 16
NSA_GROUPS = 4
NSA_HEAD_DIM = 64
NSA_BRANCHES = 3
CMP_BLOCK = 32
CMP_STRIDE = 16
SEL_BLOCK = 64
SEL_TOPN = 8
WINDOW = 256
FORCED_SCORE = 1e6

LANES = 128
SUBLANES = 8
MXU_TILE = 256
NEG = -1e30
TAKEN = -2.0
VMEM_LIMIT = 48 * 1024 * 1024

TOKEN_TILE = 512
NSA_QTILE = 256
NSA_VROWS = 80
NSA_STEP_GROUPS = 4
LOG2E = 1.4426950408889634
NT_DIMS = (((1,), (1,)), ((), ()))


def _params(n_axes):
    return pltpu.CompilerParams(dimension_semantics=("arbitrary",) * n_axes,
                                vmem_limit_bytes=VMEM_LIMIT)


def _resident(shape, index_map):
    return pl.BlockSpec(shape, index_map, pipeline_mode=pl.Buffered(1))


def _rms(x, g):
    return x * lax.rsqrt(jnp.mean(x * x, axis=-1, keepdims=True) + EPS) * g


def _gelu_tanh(x):
    return 0.5 * x * (1.0 + jnp.tanh(0.7978845608028654 * (x + 0.044715 * (x * x * x))))


def _softplus(z):
    return jnp.maximum(z, 0.0) + jnp.log1p(jnp.exp(-jnp.abs(z)))


def _sigmoid(z):
    return 0.5 * jnp.tanh(0.5 * z) + 0.5


def _sqrt_nonneg(z):
    return jnp.where(z > 0.0, z * lax.rsqrt(z), 0.0)


def _ffn_kernel(*refs, n_mix, n_chunks, final):
    x_ref = refs[0]
    a_refs = refs[1:1 + n_mix]
    wo_refs = refs[1 + n_mix:1 + 2 * n_mix]
    g_ref, wg_ref, wu_ref, wd_ref, gf_ref, o_ref = refs[1 + 2 * n_mix:]
    x = x_ref[...]
    for a_ref, wo_ref in zip(a_refs, wo_refs):
        x = x + jnp.dot(a_ref[...], wo_ref[...], preferred_element_type=F32)
    xn = _rms(x, g_ref[...]).astype(BF16)
    n_tiles = wg_ref.shape[1] // MXU_TILE
    bounds = [MXU_TILE * ((n_tiles * c + n_chunks - 1) // n_chunks) for c in range(n_chunks + 1)]
    acc = None
    for lo, hi in zip(bounds[:-1], bounds[1:]):
        gate = jnp.dot(xn, wg_ref[:, lo:hi], preferred_element_type=F32)
        up = jnp.dot(xn, wu_ref[:, lo:hi], preferred_element_type=F32)
        h = (gate * jax.nn.sigmoid(gate) * up).astype(BF16)
        part = jnp.dot(h, wd_ref[lo:hi, :], preferred_element_type=F32)
        acc = part if acc is None else acc + part
    y = x + 0.5 * acc
    if final:
        y = _rms(y, gf_ref[...])
    o_ref[...] = y


def _ffn(x, mix, g, w_gu, w_down, gf, *, layer, final):
    n, d = x.shape
    f = w_down.shape[1]
    tm = min(TOKEN_TILE, n)
    const = lambda i: (0, 0)
    row = lambda i: (i, 0)
    acts = [a for a, _ in mix]
    wos = [w for _, w in mix]
    return pl.pallas_call(
        functools.partial(_ffn_kernel, n_mix=len(mix), n_chunks=2, final=final),
        grid=(n // tm,),
        in_specs=[pl.BlockSpec((tm, d), row)]
        + [pl.BlockSpec((tm, a.shape[1]), row) for a in acts]
        + [_resident(w.shape, const) for w in wos]
        + [pl.BlockSpec((1, d), const),
           _resident((None, d, f), lambda i: (layer, 0, 0)),
           _resident((None, d, f), lambda i: (layer, 0, 1)),
           _resident((None, f, d), lambda i: (layer, 0, 0)),
           pl.BlockSpec((1, d), const)],
        out_specs=pl.BlockSpec((tm, d), row),
        out_shape=jax.ShapeDtypeStruct((n, d), F32),
        compiler_params=_params(1),
        name="ffn_final" if final else ("ffn_mix" if mix else "ffn"),
    )(x, *acts, *wos, g.reshape(1, d), w_gu, w_gu, w_down, gf.reshape(1, d))


def _norm_matmul_kernel(x_ref, g_ref, w_ref, o_ref):
    xn = _rms(x_ref[...], g_ref[...]).astype(BF16)
    o_ref[...] = jnp.dot(xn, w_ref[...], preferred_element_type=F32)


def _norm_matmul(x, g, w):
    n, d = x.shape
    dout = w.shape[1]
    tm = min(TOKEN_TILE, n)
    const = lambda i: (0, 0)
    return pl.pallas_call(
        _norm_matmul_kernel,
        grid=(n // tm,),
        in_specs=[pl.BlockSpec((tm, d), lambda i: (i, 0)),
                  pl.BlockSpec((1, d), const),
                  _resident((d, dout), const)],
        out_specs=pl.BlockSpec((tm, dout), lambda i: (i, 0)),
        out_shape=jax.ShapeDtypeStruct((n, dout), F32),
        compiler_params=_params(1),
        name="norm_matmul",
    )(x, g.reshape(1, d), w)


def _causal_conv(x, cw, cb, pad_ref):
    s = x.shape[0]
    pad_ref[0:SUBLANES, :] = jnp.zeros((SUBLANES, x.shape[1]), F32)
    pad_ref[SUBLANES:SUBLANES + s, :] = x
    y = cb + cw[CONV_WIDTH - 1:CONV_WIDTH, :] * x
    for j in range(CONV_WIDTH - 1):
        off = SUBLANES - (CONV_WIDTH - 1 - j)
        y = y + cw[j:j + 1, :] * pad_ref[off:off + s, :]
    return y


def _causal_conv_blocks(x, cw, cb, pad_ref, emit, block=256):
    s = x.shape[0]
    pad_ref[0:SUBLANES, :] = jnp.zeros((SUBLANES, x.shape[1]), F32)
    pad_ref[SUBLANES:SUBLANES + s, :] = x
    for r0 in range(0, s, block):
        y = cb + cw[CONV_WIDTH - 1:CONV_WIDTH, :] * pad_ref[SUBLANES + r0:SUBLANES + r0 + block, :]
        for j in range(CONV_WIDTH - 1):
            off = r0 + SUBLANES - (CONV_WIDTH - 1 - j)
            y = y + cw[j:j + 1, :] * pad_ref[off:off + block, :]
        emit(r0, y)


def _rglru_kernel(xa_ref, ga_ref, cw_ref, cb_ref, wr_ref, br_ref, wi_ref, bi_ref, lam_ref,
                  o_ref, pad_ref, a_ref, u_ref):
    s = xa_ref.shape[1]
    ng = s // SUBLANES
    xc = _causal_conv(xa_ref[0], cw_ref[...], cb_ref[...], pad_ref)
    xb = xc.astype(BF16)
    r = _sigmoid(jnp.dot(xb, wr_ref[...], preferred_element_type=F32) + br_ref[...])
    ig = _sigmoid(jnp.dot(xb, wi_ref[...], preferred_element_type=F32) + bi_ref[...])
    log_a = -RG_C * r * _softplus(-lam_ref[...])
    a = jnp.exp(log_a)
    th = jnp.tanh(log_a)
    u = _sqrt_nonneg(-2.0 * th / (1.0 - th)) * (ig * xc)
    a3 = a.reshape(ng, SUBLANES, LANES)
    u3 = u.reshape(ng, SUBLANES, LANES)
    row = lax.broadcasted_iota(jnp.int32, (ng, SUBLANES, LANES), 1)
    sh = 1
    while sh < SUBLANES:
        a_s = pltpu.roll(a3, sh, 1)
        u_s = pltpu.roll(u3, sh, 1)
        m = row >= sh
        u3 = jnp.where(m, a3 * u_s + u3, u3)
        a3 = jnp.where(m, a3 * a_s, a3)
        sh *= 2
    a_ref[...] = a3
    u_ref[...] = u3

    def body(i, h):
        hh = a_ref[i] * h + u_ref[i]
        u_ref[i] = hh
        return hh[SUBLANES - 1:SUBLANES, :]

    lax.fori_loop(0, ng, body, jnp.zeros((1, LANES), F32), unroll=True)
    o_ref[0] = (_gelu_tanh(ga_ref[0]) * u_ref[...].reshape(s, LANES)).astype(o_ref.dtype)


def _rglru(proj3, cw, cb, wr_bd, br, wi_bd, bi, lam):
    bsz, s, _ = proj3.shape
    c = cw.shape[1]
    nb = c // LANES
    vec = lambda b, j: (0, j)
    return pl.pallas_call(
        _rglru_kernel,
        grid=(bsz, nb),
        in_specs=[pl.BlockSpec((1, s, LANES), lambda b, j: (b, 0, j)),
                  pl.BlockSpec((1, s, LANES), lambda b, j: (b, 0, nb + j)),
                  pl.BlockSpec((CONV_WIDTH, LANES), vec),
                  pl.BlockSpec((1, LANES), vec),
                  pl.BlockSpec((LANES, LANES), lambda b, j: (j, j)),
                  pl.BlockSpec((1, LANES), vec),
                  pl.BlockSpec((LANES, LANES), lambda b, j: (j, j)),
                  pl.BlockSpec((1, LANES), vec),
                  pl.BlockSpec((1, LANES), vec)],
        out_specs=pl.BlockSpec((1, s, LANES), lambda b, j: (b, 0, j)),
        out_shape=jax.ShapeDtypeStruct((bsz, s, c), BF16),
        scratch_shapes=[pltpu.VMEM((s + SUBLANES, LANES), F32),
                        pltpu.VMEM((s // SUBLANES, SUBLANES, LANES), F32),
                        pltpu.VMEM((s // SUBLANES, SUBLANES, LANES), F32)],
        compiler_params=_params(2),
        name="rglru",
    )(proj3, proj3, cw, cb.reshape(1, c), wr_bd, br.reshape(1, c), wi_bd, bi.reshape(1, c),
      lam.reshape(1, c))


def _mlstm_kernel(bias_ref, q_ref, k_ref, v_ref, og_ref, gi_ref, gf_ref, cwq_ref, cbq_ref,
                  cwk_ref, cbk_ref, ng_ref, o_ref, pad_ref, q_sc, k_sc, li_sc, lf_sc, bc_sc):
    hd = pl.program_id(1)
    s = q_ref.shape[1]
    dh = q_ref.shape[2]
    nc = s // ML_CHUNK

    def put_q(r0, y):
        q_sc[r0:r0 + y.shape[0], :] = jax.nn.silu(y) * (dh ** -0.5)

    def put_k(r0, y):
        k_sc[r0:r0 + y.shape[0], :] = jax.nn.silu(y)

    _causal_conv_blocks(q_ref[0], cwq_ref[...], cbq_ref[...], pad_ref, put_q)
    _causal_conv_blocks(k_ref[0], cwk_ref[...], cbk_ref[...], pad_ref, put_k)
    li_sc[...] = gi_ref[0, 0] + bias_ref[hd]
    lf = -_softplus(-(gf_ref[0, 0] + bias_ref[ML_HEADS + hd]))
    lf_sc[...] = lf
    jj = lax.broadcasted_iota(jnp.int32, (ML_CHUNK, ML_CHUNK), 0)
    kk = lax.broadcasted_iota(jnp.int32, (ML_CHUNK, ML_CHUNK), 1)
    tri = kk <= jj
    upper = jnp.where(jj <= kk, 1.0, 0.0).astype(F32)
    bc_sc[...] = jnp.dot(lf, upper, precision=HIGHEST, preferred_element_type=F32)
    norm_g = ng_ref[...]

    def chunk(c, carry):
        c_st, n_st, m_st = carry
        r0 = pl.multiple_of(c * ML_CHUNK, ML_CHUNK)
        qc = q_sc[pl.ds(r0, ML_CHUNK), :]
        kc = k_sc[pl.ds(r0, ML_CHUNK), :]
        vc = v_ref[0, pl.ds(r0, ML_CHUNK), :]
        li_row = li_sc[pl.ds(c, 1), :]
        lf_row = lf_sc[pl.ds(c, 1), :]
        b_row = bc_sc[pl.ds(c, 1), :]
        b_col = jnp.sum(jnp.where(tri, lf_row, 0.0), axis=1, keepdims=True)
        g_tot = jnp.sum(lf_row, axis=1, keepdims=True)
        w_row = g_tot - b_row + li_row
        m_loc = jnp.max(w_row, axis=1, keepdims=True)
        wk = jnp.exp(w_row - m_loc)
        lhs = jnp.concatenate([vc.T * wk, jnp.broadcast_to(wk, (2 * SUBLANES, ML_CHUNK))], axis=0)
        cn = jnp.dot(lhs.astype(BF16), kc.astype(BF16), preferred_element_type=F32)
        c_loc = cn[0:dh]
        n_loc = cn[dh:dh + 1]
        d = jnp.where(tri, b_col - b_row + li_row, NEG)
        m_inter = b_col + m_st
        m = jnp.maximum(m_inter, jnp.max(d, axis=1, keepdims=True))
        qb = qc.astype(BF16)
        qk = lax.dot_general(qb, kc.astype(BF16), NT_DIMS, preferred_element_type=F32)
        p = jnp.exp(d - m) * qk
        sc = jnp.exp(m_inter - m)
        inter = lax.dot_general(qb, c_st.astype(BF16), NT_DIMS, preferred_element_type=F32)
        num = sc * inter + jnp.dot(p.astype(BF16), vc.astype(BF16), preferred_element_type=F32)
        den = jnp.sum(sc * (qc * n_st) + p, axis=1, keepdims=True)
        h = num / jnp.maximum(jnp.abs(den), jnp.exp(-m))
        h = h * lax.rsqrt(jnp.mean(h * h, axis=1, keepdims=True) + EPS) * norm_g
        o_ref[0, pl.ds(r0, ML_CHUNK), :] = (
            jax.nn.sigmoid(og_ref[0, pl.ds(r0, ML_CHUNK), :]) * h).astype(o_ref.dtype)
        m_new = jnp.maximum(g_tot + m_st, m_loc)
        sa = jnp.exp(g_tot + m_st - m_new)
        sb = jnp.exp(m_loc - m_new)
        return sa * c_st + sb * c_loc, sa * n_st + sb * n_loc, m_new

    init = (jnp.zeros((dh, dh), F32), jnp.zeros((1, dh), F32), jnp.full((1, 1), NEG, F32))
    lax.fori_loop(0, nc, chunk, init, unroll=True)


def _mlstm(proj3, gates_t, bias, cw, cb, norm_g, *, col0):
    bsz, s, _ = proj3.shape
    nh = ML_HEADS
    dh = LANES
    nc = s // ML_CHUNK
    base = col0 // dh
    blk = lambda off: pl.BlockSpec((1, s, dh), lambda b, h: (b, 0, base + off + h))
    vec = lambda off: (lambda b, h: (0, off + h))
    return pl.pallas_call(
        _mlstm_kernel,
        grid=(bsz, nh),
        in_specs=[pl.BlockSpec(memory_space=pltpu.SMEM),
                  blk(0), blk(nh), blk(2 * nh), blk(3 * nh),
                  pl.BlockSpec((1, 1, nc, ML_CHUNK), lambda b, h: (b, h, 0, 0)),
                  pl.BlockSpec((1, 1, nc, ML_CHUNK), lambda b, h: (b, nh + h, 0, 0)),
                  pl.BlockSpec((CONV_WIDTH, dh), vec(0)), pl.BlockSpec((1, dh), vec(0)),
                  pl.BlockSpec((CONV_WIDTH, dh), vec(nh)), pl.BlockSpec((1, dh), vec(nh)),
                  pl.BlockSpec((1, dh), vec(0))],
        out_specs=pl.BlockSpec((1, s, dh), lambda b, h: (b, 0, h)),
        out_shape=jax.ShapeDtypeStruct((bsz, s, nh * dh), BF16),
        scratch_shapes=[pltpu.VMEM((s + SUBLANES, dh), F32),
                        pltpu.VMEM((s, dh), F32), pltpu.VMEM((s, dh), F32),
                        pltpu.VMEM((nc, ML_CHUNK), F32), pltpu.VMEM((nc, ML_CHUNK), F32),
                        pltpu.VMEM((nc, ML_CHUNK), F32)],
        compiler_params=_params(2),
        name="mlstm",
    )(bias, proj3, proj3, proj3, proj3, gates_t, gates_t, cw, cb.reshape(1, -1), cw,
      cb.reshape(1, -1), norm_g.reshape(1, -1))


def _ab_mixer(x, g, w_in, rg_conv_w, rg_conv_b, rg_w_r, rg_b_r, rg_w_i, rg_b_i, rg_lambda,
              ml_conv_w, ml_conv_b, ml_b_i, ml_b_f, ml_norm, w_out, bsz, s):
    d_rg = rg_conv_w.shape[1]
    d_ml = ml_norm.shape[0]
    d_main = 2 * d_rg + 4 * d_ml
    n_gate = 2 * ML_HEADS
    w_in_p = jnp.pad(w_in, ((0, 0), (0, LANES - n_gate))).astype(BF16)
    proj = _norm_matmul(x, g, w_in_p)
    proj3 = proj.reshape(bsz, s, proj.shape[1])
    bd = lambda w: jax.scipy.linalg.block_diag(*[w[i] for i in range(RG_BLOCKS)]).astype(BF16)
    ya = _rglru(proj3, rg_conv_w, rg_conv_b, bd(rg_w_r), rg_b_r, bd(rg_w_i), rg_b_i, rg_lambda)
    gates_t = proj3[:, :, d_main:d_main + n_gate].transpose(0, 2, 1).reshape(
        bsz, n_gate, s // ML_CHUNK, ML_CHUNK)
    yb = _mlstm(proj3, gates_t, jnp.concatenate([ml_b_i, ml_b_f]), ml_conv_w, ml_conv_b, ml_norm,
                col0=2 * d_rg)
    w_out_b = w_out.astype(BF16)
    return [(ya.reshape(bsz * s, d_rg), w_out_b[:d_rg]), (yb.reshape(bsz * s, d_ml), w_out_b[d_rg:])]


def _nsa_proj_kernel(x_ref, g_ref, w_ref, q_ref, gt_ref, kc_ref, vc_ref, ks_ref, kw_ref, vs_ref, vw_ref,
                     *, seq):
    tm = x_ref.shape[0]
    kv = NSA_GROUPS * NSA_HEAD_DIM
    width = NSA_HEADS * NSA_HEAD_DIM
    half = LANES // 2
    xn = _rms(x_ref[...], g_ref[...]).astype(BF16)
    proj = jnp.dot(xn, w_ref[...], preferred_element_type=F32)
    lane = lax.broadcasted_iota(jnp.int32, (tm, LANES), 1)
    lo = lane < half

    def lane_pair(c0):
        p = proj[:, c0:c0 + LANES]
        return p, pltpu.roll(p, half, 1)

    scale = NSA_HEAD_DIM ** -0.5 * LOG2E
    for pair in range(NSA_HEADS // 2):
        for j, piece in enumerate(lane_pair(pair * LANES)):
            h = 2 * pair + j
            q_ref[:, h * LANES:(h + 1) * LANES] = jnp.where(lo, piece * scale, 0.0).astype(BF16)
    gt_ref[0] = proj[:, width + 6 * kv:width + 6 * kv + LANES].T
    kc_ref[...] = proj[:, width:width + kv]
    vc_ref[...] = proj[:, width + kv:width + 2 * kv]
    pos = (pl.program_id(0) % (seq // tm)) * tm + lax.broadcasted_iota(jnp.int32, (tm, LANES), 0)
    onehot = jnp.where(lane == half + pos // SEL_BLOCK, 1.0, 0.0)
    for pair in range(NSA_GROUPS // 2):
        for idx, ref, fill in ((2, ks_ref, onehot), (4, kw_ref, 0.0)):
            for j, piece in enumerate(lane_pair(width + idx * kv + pair * LANES)):
                ref[0, 2 * pair + j] = jnp.where(lo, piece, fill).astype(BF16)
        for idx, ref in ((3, vs_ref), (5, vw_ref)):
            c0 = width + idx * kv + pair * LANES
            t = proj[:, c0:c0 + LANES].T.astype(BF16)
            ones = jnp.ones((NSA_VROWS - half, NSA_QTILE), BF16)
            for j in range(tm // NSA_QTILE):
                for hf in range(2):
                    ref[0, 2 * pair + hf, j, 0:half, :] = t[hf * half:(hf + 1) * half,
                                                            j * NSA_QTILE:(j + 1) * NSA_QTILE]
                    ref[0, 2 * pair + hf, j, half:NSA_VROWS, :] = ones


def _nsa_proj(x, g, w, bsz, s):
    n, d = x.shape
    dout = w.shape[1]
    tm = min(TOKEN_TILE, s)
    nsb = s // tm
    ng, dk = NSA_GROUPS, NSA_HEAD_DIM
    kv = ng * dk
    tq = NSA_QTILE
    const = lambda i: (0, 0)
    row = lambda i: (i, 0)
    keyed = lambda i: (i // nsb, 0, i % nsb, 0)
    return pl.pallas_call(
        functools.partial(_nsa_proj_kernel, seq=s),
        grid=(n // tm,),
        in_specs=[pl.BlockSpec((tm, d), row), pl.BlockSpec((1, d), const), _resident((d, dout), const)],
        out_specs=[pl.BlockSpec((tm, NSA_HEADS * LANES), row),
                   pl.BlockSpec((1, LANES, tm), lambda i: (i // nsb, 0, i % nsb)),
                   pl.BlockSpec((tm, kv), row), pl.BlockSpec((tm, kv), row),
                   pl.BlockSpec((1, ng, tm, LANES), keyed), pl.BlockSpec((1, ng, tm, LANES), keyed),
                   pl.BlockSpec((1, ng, tm // tq, NSA_VROWS, tq), lambda i: (i // nsb, 0, i % nsb, 0, 0)),
                   pl.BlockSpec((1, ng, tm // tq, NSA_VROWS, tq), lambda i: (i // nsb, 0, i % nsb, 0, 0))],
        out_shape=[jax.ShapeDtypeStruct((n, NSA_HEADS * LANES), BF16),
                   jax.ShapeDtypeStruct((bsz, LANES, s), F32),
                   jax.ShapeDtypeStruct((n, kv), F32), jax.ShapeDtypeStruct((n, kv), F32),
                   jax.ShapeDtypeStruct((bsz, ng, s, LANES), BF16),
                   jax.ShapeDtypeStruct((bsz, ng, s, LANES), BF16),
                   jax.ShapeDtypeStruct((bsz, ng, s // tq, NSA_VROWS, tq), BF16),
                   jax.ShapeDtypeStruct((bsz, ng, s // tq, NSA_VROWS, tq), BF16)],
        compiler_params=_params(1),
        name="nsa_proj",
    )(x, g.reshape(1, d), w)


def _compress_kernel(x_ref, pe_ref, w1_ref, b1_ref, w2_ref, o_ref, *, transpose_out):
    nb = x_ref.shape[1] // CMP_STRIDE
    top = bot = None
    for l in range(CMP_STRIDE):
        rows = x_ref[0, pl.ds(l, nb, stride=CMP_STRIDE), :]
        t = jnp.dot((rows + pe_ref[l:l + 1, :]).astype(BF16), w1_ref[l], preferred_element_type=F32)
        b = jnp.dot((rows + pe_ref[CMP_STRIDE + l:CMP_STRIDE + l + 1, :]).astype(BF16),
                    w1_ref[CMP_STRIDE + l], preferred_element_type=F32)
        top = t if top is None else top + t
        bot = b if bot is None else bot + b
    pre = top + pltpu.roll(bot, nb - 1, 0) + b1_ref[...]
    out = jnp.dot(_gelu_tanh(pre).astype(BF16), w2_ref[...], preferred_element_type=F32)
    out = jnp.where(lax.broadcasted_iota(jnp.int32, out.shape, 0) < nb - 1, out, 0.0)
    for hf in range(2):
        piece = out[:, hf * LANES:(hf + 1) * LANES]
        if transpose_out:
            o_ref[0, hf] = piece.T[0:NSA_HEAD_DIM, :].astype(o_ref.dtype)
        else:
            o_ref[0, hf] = piece.astype(o_ref.dtype)


def _compress(kv_in, pe, w1, b1, w2, *, transpose_out):
    bsz, s, _ = kv_in.shape
    dk, hid = NSA_HEAD_DIM, w1.shape[1]
    nb = s // CMP_STRIDE
    w1r = w1.reshape(CMP_BLOCK, dk, hid)
    z = jnp.zeros_like(w1r)
    w1p = jnp.concatenate([jnp.concatenate([w1r, z], axis=2), jnp.concatenate([z, w1r], axis=2)],
                          axis=1).astype(BF16)
    w2w = jnp.pad(w2, ((0, 0), (0, LANES - dk)))
    z2 = jnp.zeros_like(w2w)
    w2p = jnp.concatenate([jnp.concatenate([w2w, z2], axis=1), jnp.concatenate([z2, w2w], axis=1)],
                          axis=0).astype(BF16)
    const2 = lambda b, p: (0, 0)
    out_block = (1, 2, dk, nb) if transpose_out else (1, 2, nb, LANES)
    out_shape = (bsz, NSA_GROUPS, dk, nb) if transpose_out else (bsz, NSA_GROUPS, nb, LANES)
    return pl.pallas_call(
        functools.partial(_compress_kernel, transpose_out=transpose_out),
        grid=(bsz, NSA_GROUPS // 2),
        in_specs=[pl.BlockSpec((1, s, LANES), lambda b, p: (b, 0, p)),
                  pl.BlockSpec((CMP_BLOCK, LANES), const2),
                  _resident((CMP_BLOCK, LANES, 2 * hid), lambda b, p: (0, 0, 0)),
                  pl.BlockSpec((1, 2 * hid), const2),
                  _resident((2 * hid, 2 * LANES), const2)],
        out_specs=pl.BlockSpec(out_block, lambda b, p: (b, p, 0, 0)),
        out_shape=jax.ShapeDtypeStruct(out_shape, BF16),
        compiler_params=_params(2),
        name="nsa_compress",
    )(kv_in, jnp.concatenate([pe, pe], axis=1), w1p, jnp.concatenate([b1, b1]).reshape(1, 2 * hid), w2p)


def _nsa_kernel(bg_ref, q_ref, gt_ref, kc_ref, vc_ref, ks_ref, vs_ref, kw_ref, vw_ref, msel_ref,
                o_ref, qa_sc, s_sc, bias_sc, m_sc, acc_sc, part_sc):
    gp = pl.program_id(1)
    qi = pl.program_id(2)
    tq = q_ref.shape[0]
    rep = NSA_HEADS // NSA_GROUPS
    dk = NSA_HEAD_DIM
    half = LANES // 2
    n_sel = 32
    units = range(NSA_STEP_GROUPS)
    heads = range(rep)
    s0 = qi * tq
    key_off = lax.broadcasted_iota(jnp.int32, (tq, tq), 0)
    qry_off = lax.broadcasted_iota(jnp.int32, (tq, tq), 1)
    cols = lambda r: slice(r * tq, (r + 1) * tq)
    nt_dot = lambda k, q: lax.dot_general(k, q, NT_DIMS, preferred_element_type=F32)
    q_head = lambda u, r: q_ref[:, (u * rep + r) * LANES:(u * rep + r + 1) * LANES]

    def gate_row(u, r, br):
        idx = (gp * NSA_STEP_GROUPS + u) * (rep * NSA_BRANCHES) + r * NSA_BRANCHES + br
        return jax.nn.sigmoid(gt_ref[0, pl.ds(idx, 1), :] + bg_ref[idx])

    kp = jnp.maximum(qi - 1, 0)
    p0 = pl.multiple_of(kp * tq, tq)
    d0 = pl.multiple_of(qi * tq, tq)
    cmp_scores = [[nt_dot(kc_ref[0, u], q_head(u, r)) for r in heads] for u in units]
    win_scores = [[(nt_dot(kw_ref[0, u, pl.ds(p0, tq), :], q_head(u, r)),
                    nt_dot(kw_ref[0, u, pl.ds(d0, tq), :], q_head(u, r))) for r in heads] for u in units]
    for u in units:
        for r in heads:
            s_sc[1, u, r] = nt_dot(ks_ref[0, u, pl.ds(d0, tq), :], q_head(u, r))

    n_blk = kc_ref.shape[2]
    blk_id = lax.broadcasted_iota(jnp.int32, (n_blk, tq), 0)
    t_blk = s0 + lax.broadcasted_iota(jnp.int32, (n_blk, tq), 1)
    cmask = (blk_id * CMP_STRIDE + (CMP_BLOCK - 1) <= t_blk) & (blk_id < n_blk - 1)
    psum = [None] * NSA_STEP_GROUPS
    for r in heads:
        for u in units:
            scm = jnp.where(cmask, cmp_scores[u][r], NEG)
            e = jnp.where(cmask, jnp.exp2(scm - jnp.max(scm, axis=0, keepdims=True)), 0.0)
            den = jnp.sum(e, axis=0, keepdims=True)
            p = e * (1.0 / jnp.where(den > 0.0, den, 1.0))
            psum[u] = p if psum[u] is None else psum[u] + p
            part_sc[u, :, cols(r)] = gate_row(u, r, 0) * jnp.dot(vc_ref[0, u], p.astype(BF16),
                                                                 preferred_element_type=F32)

    jb = lax.broadcasted_iota(jnp.int32, (n_sel, tq), 0)
    jbf = jb.astype(F32)
    cur = (s0 + lax.broadcasted_iota(jnp.int32, (n_sel, tq), 1)) // SEL_BLOCK
    valid = jb <= cur
    forced = ((jb == 0) | (jb == cur) | (jb == cur - 1)) & valid
    for u in units:
        p_hi = psum[u].astype(BF16)
        p_lo = (psum[u] - p_hi.astype(F32)).astype(BF16)
        score_t = (jnp.dot(msel_ref[...], p_hi, preferred_element_type=F32)
                   + jnp.dot(msel_ref[...], p_lo, preferred_element_type=F32))
        s2 = jnp.where(forced, FORCED_SCORE, jnp.where(valid, score_t[half:half + n_sel, :], -1.0))
        for _ in range(SEL_TOPN):
            best = jnp.max(s2, axis=0, keepdims=True)
            first = jnp.min(jnp.where(s2 == best, jbf, float(n_sel)), axis=0, keepdims=True)
            s2 = jnp.where(jbf == first, TAKEN, s2)
        bias_t = jnp.where((s2 == TAKEN) & valid, 0.0, NEG)
        bias_sc[u] = bias_t
        bias_full = jnp.concatenate(
            [jnp.zeros((half, tq), F32), bias_t, jnp.zeros((LANES - half - n_sel, tq), F32)], axis=0)
        bias_q = bias_full.T
        for r in heads:
            qa_sc[u, r] = (q_head(u, r).astype(F32) + bias_q).astype(BF16)

    prev_mask = (key_off > qry_off) & (qi >= 1)
    diag_mask = key_off <= qry_off
    for r in heads:
        for u in units:
            s_prev = jnp.where(prev_mask, win_scores[u][r][0], NEG)
            s_diag = jnp.where(diag_mask, win_scores[u][r][1], NEG)
            mw = jnp.maximum(jnp.max(s_prev, axis=0, keepdims=True), jnp.max(s_diag, axis=0, keepdims=True))
            win = (jnp.dot(vw_ref[0, u, kp], jnp.exp2(s_prev - mw).astype(BF16), preferred_element_type=F32)
                   + jnp.dot(vw_ref[0, u, qi], jnp.exp2(s_diag - mw).astype(BF16),
                             preferred_element_type=F32))
            part_sc[u, :, cols(r)] += gate_row(u, r, 2) * (win[0:dk] * (1.0 / win[dk:dk + 1]))

    def qk_into(kt, slot):
        k0 = pl.multiple_of(kt * tq, tq)
        for u in units:
            k_tile = ks_ref[0, u, pl.ds(k0, tq), :]
            for r in heads:
                s_sc[slot, u, r] = nt_dot(k_tile, qa_sc[u, r])

    qk_into(0, 0)
    blk0 = qi * (tq // SEL_BLOCK)
    for u in units:
        bias_diag = jnp.concatenate(
            [jnp.broadcast_to(bias_sc[u, pl.ds(blk0 + jj, 1), :], (SEL_BLOCK, tq))
             for jj in range(tq // SEL_BLOCK)], axis=0)
        for r in heads:
            s = jnp.where(diag_mask, s_sc[1, u, r] + bias_diag, NEG)
            m0 = jnp.max(s, axis=0, keepdims=True)
            m_sc[u, :, cols(r)] = m0
            acc_sc[u, :, cols(r)] = jnp.dot(vs_ref[0, u, qi], jnp.exp2(s - m0).astype(BF16),
                                            preferred_element_type=F32)

    def softmax_pv(kt, slot):
        for r in heads:
            for u in units:
                v_t = vs_ref[0, u, kt]
                s = s_sc[slot, u, r]
                m_old = m_sc[u, :, cols(r)]
                m_new = jnp.maximum(m_old, jnp.max(s, axis=0, keepdims=True))
                alpha = jnp.exp2(m_old - m_new)
                p = jnp.exp2(s - m_new)
                acc_sc[u, :, cols(r)] = alpha * acc_sc[u, :, cols(r)] + jnp.dot(
                    v_t, p.astype(BF16), preferred_element_type=F32)
                m_sc[u, :, cols(r)] = m_new


    def pair_body(j, carry):
        a = 2 * j
        qk_into(a + 1, 1)
        softmax_pv(a, 0)
        qk_into(a + 2, 0)
        softmax_pv(a + 1, 1)
        return carry

    lax.fori_loop(0, qi // 2, pair_body, 0)

    @pl.when(qi % 2 == 1)
    def _():
        softmax_pv(qi - 1, 0)

    for u in units:
        merged = []
        for r in heads:
            o_slc = acc_sc[u, 0:dk, cols(r)] * (1.0 / acc_sc[u, dk:dk + 1, cols(r)])
            merged.append(part_sc[u, :, cols(r)] + gate_row(u, r, 1) * o_slc)
        for pair in range(rep // 2):
            both = jnp.concatenate([merged[2 * pair], merged[2 * pair + 1]], axis=0)
            c0 = (u * (rep // 2) + pair) * LANES
            o_ref[:, c0:c0 + LANES] = both.T.astype(o_ref.dtype)


def _nsa_attention(q, gt, b_gate, kc, vc_t, ks, vs_t, kw, vw_t, msel_t, bsz, s):
    n = q.shape[0]
    tq = NSA_QTILE
    assert tq == WINDOW and s % tq == 0 and s // SEL_BLOCK == 32
    rep = NSA_HEADS // NSA_GROUPS
    dk = NSA_HEAD_DIM
    nq = s // tq
    ug = NSA_STEP_GROUPS
    per_bg = lambda a: pl.BlockSpec((1, ug) + a.shape[2:], lambda b, g, i: (b, g) + (0,) * (a.ndim - 2))
    return pl.pallas_call(
        _nsa_kernel,
        grid=(bsz, NSA_GROUPS // ug, nq),
        in_specs=[pl.BlockSpec(memory_space=pltpu.SMEM),
                  pl.BlockSpec((tq, ug * rep * LANES), lambda b, g, i: (b * nq + i, g)),
                  pl.BlockSpec((1, LANES, tq), lambda b, g, i: (b, 0, i)),
                  per_bg(kc), per_bg(vc_t), per_bg(ks), per_bg(vs_t), per_bg(kw), per_bg(vw_t),
                  pl.BlockSpec(msel_t.shape, lambda b, g, i: (0, 0))],
        out_specs=pl.BlockSpec((tq, ug * rep * dk), lambda b, g, i: (b * nq + i, g)),
        out_shape=jax.ShapeDtypeStruct((n, NSA_HEADS * dk), BF16),
        scratch_shapes=[pltpu.VMEM((ug, rep, tq, LANES), BF16),
                        pltpu.VMEM((2, ug, rep, tq, tq), F32),
                        pltpu.VMEM((ug, s // SEL_BLOCK, tq), F32),
                        pltpu.VMEM((ug, 1, rep * tq), F32),
                        pltpu.VMEM((ug, NSA_VROWS, rep * tq), F32),
                        pltpu.VMEM((ug, dk, rep * tq), F32)],
        compiler_params=_params(3),
        name="nsa_attention",
    )(b_gate, q, gt, kc, vc_t, ks, vs_t, kw, vw_t, msel_t)


def _sel_matrix_t(n_rows, n_sel):
    n_cmp = np.arange(n_rows)[None, :] * CMP_STRIDE
    sel = np.arange(n_sel)[:, None] * SEL_BLOCK
    ov = np.clip(np.minimum(n_cmp + CMP_BLOCK, sel + SEL_BLOCK) - np.maximum(n_cmp, sel), 0, None)
    out = np.zeros((LANES, n_rows), np.float32)
    out[LANES // 2:LANES // 2 + n_sel] = ov / CMP_BLOCK
    return jnp.asarray(out, dtype=BF16)


def _nsa_mixer(x, g, w_in, pe_k, k_w1, k_b1, k_w2, pe_v, v_w1, v_b1, v_w2, b_gate, w_out, bsz, s):
    n_gate = NSA_BRANCHES * NSA_HEADS
    kv = NSA_GROUPS * NSA_HEAD_DIM
    w_in_p = jnp.pad(w_in, ((0, 0), (0, LANES - n_gate))).astype(BF16)
    q, gt, kc_in, vc_in, ks, kw, vs_t, vw_t = _nsa_proj(x, g, w_in_p, bsz, s)
    kc = _compress(kc_in.reshape(bsz, s, kv), pe_k, k_w1, k_b1, k_w2, transpose_out=False)
    vc_t = _compress(vc_in.reshape(bsz, s, kv), pe_v, v_w1, v_b1, v_w2, transpose_out=True)
    msel_t = _sel_matrix_t(s // CMP_STRIDE, s // SEL_BLOCK)
    attn = _nsa_attention(q, gt, b_gate, kc, vc_t, ks, vs_t, kw, vw_t, msel_t, bsz, s)
    return [(attn, w_out.astype(BF16))]


def kernel(x, ffn1_norm, ffn1_w_gu, ffn1_w_down, mix_norm, ffn2_norm, ffn2_w_gu, ffn2_w_down,
           ab_w_in, rg_conv_w, rg_conv_b, rg_w_r, rg_b_r, rg_w_i, rg_b_i, rg_lambda,
           ml_conv_w, ml_conv_b, ml_b_i, ml_b_f, ml_norm, ab_w_out,
           nsa_w_in, nsa_pe_k, nsa_k_w1, nsa_k_b1, nsa_k_w2, nsa_pe_v, nsa_v_w1, nsa_v_b1, nsa_v_w2,
           nsa_b_gate, nsa_w_out, final_norm):
    bsz, s, d = x.shape
    depth = ffn1_norm.shape[0]
    h = x.reshape(bsz * s, d)
    w1_gu, w1_down = ffn1_w_gu.astype(BF16), ffn1_w_down.astype(BF16)
    w2_gu, w2_down = ffn2_w_gu.astype(BF16), ffn2_w_down.astype(BF16)
    for i in range(depth):
        j = i // 2
        h = _ffn(h, [], ffn1_norm[i], w1_gu, w1_down, final_norm, layer=i, final=False)
        if i % 2 == 0:
            mix = _ab_mixer(h, mix_norm[i], ab_w_in[j], rg_conv_w[j], rg_conv_b[j], rg_w_r[j], rg_b_r[j],
                            rg_w_i[j], rg_b_i[j], rg_lambda[j], ml_conv_w[j], ml_conv_b[j], ml_b_i[j],
                            ml_b_f[j], ml_norm[j], ab_w_out[j], bsz, s)
        else:
            mix = _nsa_mixer(h, mix_norm[i], nsa_w_in[j], nsa_pe_k[j], nsa_k_w1[j], nsa_k_b1[j],
                             nsa_k_w2[j], nsa_pe_v[j], nsa_v_w1[j], nsa_v_b1[j], nsa_v_w2[j],
                             nsa_b_gate[j], nsa_w_out[j], bsz, s)
        h = _ffn(h, mix, ffn2_norm[i], w2_gu, w2_down, final_norm, layer=i, final=(i == depth - 1))
    return h.reshape(bsz, s, d)
```
